```python
import math, functools
import jax, jax.numpy as jnp
from jax import lax
import numpy as np

D_MODEL = 1024
BATCH = 4
SEQ = 4096
DEPTH = 4
DEC_BATCH = 128
DEC_SEQ = 8
PAST_LEN = 2048
PAGE_SIZE = 128

N_META = 16
D_MIX = D_MODEL
D_GROUP = D_MIX // 4
N_HEADS = 4
HEAD_DIM = D_GROUP // N_HEADS
IDX_HEADS = 4
IDX_DIM = 64
TOPK_MAX = 256
QBLOCK = 128
N_BUCKETS = 32
MAX_DISTANCE = 128
SSM_CH = D_GROUP
SSM_GROUP_CH = 16
SSM_GROUPS = SSM_CH // SSM_GROUP_CH
SSM_STATE = 64
CONV_CH = D_GROUP
CONV_WIDTH = 31
POOL_CH = D_GROUP
POOL_WINDOWS = (2, 4, 8, 16)
POOL_GROUP_CH = POOL_CH // len(POOL_WINDOWS)
POOL_BUF = max(POOL_WINDOWS) - 1
D_FF = 4 * D_MODEL
EPS = 1e-6
COL_SPLITS = (D_GROUP, D_GROUP, D_GROUP, IDX_HEADS * IDX_DIM, IDX_DIM, IDX_HEADS, SSM_CH, CONV_CH, CONV_CH, POOL_CH)
D_IN = sum(COL_SPLITS)

kernel_name = "hymba_dsa_s5_conv_pool_decoder_step"


def rmsnorm(x, g):
    xf = x.astype(jnp.float32)
    y = xf * lax.rsqrt(jnp.mean(xf * xf, axis=-1, keepdims=True) + EPS)
    return (y * g.astype(jnp.float32)).astype(x.dtype)


def layernorm(x, g, b):
    xf = x.astype(jnp.float32)
    mu = jnp.mean(xf, axis=-1, keepdims=True)
    var = jnp.mean(jnp.square(xf - mu), axis=-1, keepdims=True)
    y = (xf - mu) * lax.rsqrt(var + EPS) * g.astype(jnp.float32) + b.astype(jnp.float32)
    return y.astype(x.dtype)


def half_ffn(x, g, w_gate, w_up, w_down):
    h = rmsnorm(x, g)
    return x + 0.5 * ((jax.nn.silu(h @ w_gate) * (h @ w_up)) @ w_down)


def split_cols(z):
    offs = [int(o) for o in np.cumsum(COL_SPLITS)[:-1]]
    return jnp.split(z, offs, axis=-1)


def rel_bucket(dist):
    n = jnp.maximum(dist, 0)
    max_exact = N_BUCKETS // 2
    nf = jnp.maximum(n, 1).astype(jnp.float32)
    large = max_exact + (jnp.log(nf / max_exact) / math.log(MAX_DISTANCE / max_exact) * (N_BUCKETS - max_exact)).astype(jnp.int32)
    large = jnp.minimum(large, N_BUCKETS - 1)
    return jnp.where(n < max_exact, n, large)


def indexer_select(qi, wi, ki, mask, n_sel):
    s = jax.nn.relu(jnp.einsum("bqhd,bkd->bhqk", qi.astype(jnp.float32), ki.astype(jnp.float32)))
    score = jnp.einsum("bqh,bhqk->bqk", wi.astype(jnp.float32), s)
    score = jnp.where(mask, score, -jnp.inf)
    top_s, idx = lax.top_k(score, n_sel)
    return idx, jnp.isfinite(top_s)


def attend_selected(q, k_sel, v_sel, valid, dist, rel_bias):
    logits = jnp.einsum("bqhd,bqkhd->bhqk", q, k_sel).astype(jnp.float32) * (HEAD_DIM ** -0.5)
    bias = rel_bias[rel_bucket(dist)].astype(jnp.float32)
    logits = logits + jnp.transpose(bias, (0, 3, 1, 2))
    logits = jnp.where(valid[:, None], logits, -jnp.inf)
    p = jax.nn.softmax(logits, axis=-1).astype(v_sel.dtype)
    return jnp.einsum("bhqk,bqkhd->bqhd", p, v_sel)


def attn_prompt(q, k, v, qi, ki, wi, rel_bias):
    B, L = q.shape[0], q.shape[1]
    n_sel = min(TOPK_MAX, L // 4)
    nb = -(-L // QBLOCK)
    lp_ = nb * QBLOCK

    def blocks(a):
        a = jnp.pad(a, [(0, 0), (0, lp_ - L)] + [(0, 0)] * (a.ndim - 2))
        return jnp.moveaxis(a.reshape((B, nb, QBLOCK) + a.shape[2:]), 1, 0)

    key_pos = jnp.arange(L)
    bidx = jnp.arange(B)[:, None, None]

    def one_block(args):
        qb, qib, wib, blk = args
        qpos = blk * QBLOCK + jnp.arange(QBLOCK)
        mask = (key_pos[None, :] <= qpos[:, None])[None]
        idx, valid = indexer_select(qib, wib, ki, mask, n_sel)
        ks = k[bidx, idx]
        vs = v[bidx, idx]
        dist = qpos[None, :, None] - idx
        return attend_selected(qb, ks, vs, valid, dist, rel_bias)

    out = lax.map(one_block, (blocks(q), blocks(qi), blocks(wi), jnp.arange(nb)))
    out = jnp.moveaxis(out, 0, 1).reshape(B, lp_, N_HEADS, HEAD_DIM)
    return out[:, :L]


def attn_sample(q, k, v, qi, ki, wi, cache_k_l, cache_v_l, cache_ki_l, page_table, rel_bias):
    DB, T = q.shape[0], q.shape[1]
    n_pages = page_table.shape[1]
    past = n_pages * PAGE_SIZE
    lk = past + T
    n_sel = min(TOPK_MAX, lk // 4)
    ki_past = cache_ki_l[page_table].reshape(DB, past, IDX_DIM)
    ki_all = jnp.concatenate([ki_past, ki.astype(ki_past.dtype)], axis=1)
    qpos = past + jnp.arange(T)
    mask = (jnp.arange(lk)[None, :] <= qpos[:, None])[None]
    idx, valid = indexer_select(qi, wi, ki_all, mask, n_sel)
    bidx = jnp.arange(DB)[:, None, None]
    is_past = (idx < past)[..., None, None]
    sp = jnp.minimum(idx, past - 1)
    phys = page_table[bidx, sp // PAGE_SIZE]
    off = sp % PAGE_SIZE
    sn = jnp.clip(idx - past, 0, T - 1)
    ks = jnp.where(is_past, cache_k_l[phys, off], k[bidx, sn].astype(cache_k_l.dtype))
    vs = jnp.where(is_past, cache_v_l[phys, off], v[bidx, sn].astype(cache_v_l.dtype))
    dist = qpos[None, :, None] - idx
    return attend_selected(q, ks, vs, valid, dist, rel_bias)


def ssm_combine(e1, e2):
    a1r, a1i, b1r, b1i = e1
    a2r, a2i, b2r, b2i = e2
    return (a1r * a2r - a1i * a2i, a1r * a2i + a1i * a2r,
            a2r * b1r - a2i * b1i + b2r, a2r * b1i + a2i * b1r + b2i)


def ssm_mixer(u, h0_re, h0_im, lp):
    B, T, _ = u.shape
    f32 = jnp.float32
    a_re = lp["ssm_a_re"].astype(f32)
    a_im = lp["ssm_a_im"].astype(f32)
    dt = jnp.exp(lp["ssm_log_dt"].astype(f32))[:, None]
    decay = jnp.exp(dt * a_re)
    ab_re = decay * jnp.cos(dt * a_im)
    ab_im = decay * jnp.sin(dt * a_im)
    den = a_re * a_re + a_im * a_im
    f_re = ((ab_re - 1.0) * a_re + ab_im * a_im) / den
    f_im = (ab_im * a_re - (ab_re - 1.0) * a_im) / den
    b_re = lp["ssm_b_re"].astype(f32)
    b_im = lp["ssm_b_im"].astype(f32)
    bb_re = f_re[..., None] * b_re - f_im[..., None] * b_im
    bb_im = f_re[..., None] * b_im + f_im[..., None] * b_re
    ug = u.astype(f32).reshape(B, T, SSM_GROUPS, SSM_GROUP_CH)
    bu_re = jnp.einsum("gpc,btgc->btgp", bb_re, ug)
    bu_im = jnp.einsum("gpc,btgc->btgp", bb_im, ug)
    h0r = h0_re.astype(f32)
    h0i = h0_im.astype(f32)
    bu_re = bu_re.at[:, 0].add(ab_re * h0r - ab_im * h0i)
    bu_im = bu_im.at[:, 0].add(ab_re * h0i + ab_im * h0r)
    ar = jnp.broadcast_to(ab_re, bu_re.shape)
    ai = jnp.broadcast_to(ab_im, bu_im.shape)
    _, _, h_re, h_im = lax.associative_scan(ssm_combine, (ar, ai, bu_re, bu_im), axis=1)
    y = (jnp.einsum("gcp,btgp->btgc", lp["ssm_c_re"].astype(f32), h_re)
         - jnp.einsum("gcp,btgp->btgc", lp["ssm_c_im"].astype(f32), h_im))
    y = y.reshape(B, T, SSM_CH) + lp["ssm_d"].astype(f32) * u.astype(f32)
    z = jax.nn.gelu(y).astype(u.dtype)
    out = z * jax.nn.sigmoid(z @ lp["ssm_w_glu"])
    return out, h_re[:, -1].astype(h0_re.dtype), h_im[:, -1].astype(h0_im.dtype)


def conv_mixer(a, g, buf, lp):
    u = a * jax.nn.sigmoid(g)
    ext = jnp.concatenate([buf.astype(u.dtype), u], axis=1)
    y = lax.conv_general_dilated(ext, lp["conv_w"][:, None, :].astype(u.dtype), (1,), "VALID",
                                 dimension_numbers=("NWC", "WIO", "NWC"),
                                 feature_group_count=CONV_CH) + lp["conv_b"]
    y = jax.nn.silu(layernorm(y, lp["conv_ln_g"], lp["conv_ln_b"]))
    return y, ext[:, ext.shape[1] - (CONV_WIDTH - 1):]


def pool_mixer(p, buf, pos0, lp):
    B, T, C = p.shape
    f32 = jnp.float32
    ext = jnp.concatenate([buf.astype(p.dtype), p], axis=1)
    ef = ext.astype(f32)
    cs = jnp.concatenate([jnp.zeros((B, 1, C), f32), jnp.cumsum(ef, axis=1)], axis=1)
    pos = pos0 + jnp.arange(T)
    cur = ef[:, POOL_BUF:]
    parts = []
    for gi, w in enumerate(POOL_WINDOWS):
        lo, hi = gi * POOL_GROUP_CH, (gi + 1) * POOL_GROUP_CH
        win = cs[:, POOL_BUF + 1:POOL_BUF + 1 + T, lo:hi] - cs[:, POOL_BUF + 1 - w:POOL_BUF + 1 - w + T, lo:hi]
        cnt = jnp.minimum(pos + 1, w).astype(f32)[None, :, None]
        parts.append(win / cnt - cur[..., lo:hi])
    d = jnp.stack(parts, axis=2)
    y = jnp.einsum("btgc,gcd->btgd", d, lp["pool_w"].astype(f32)).reshape(B, T, C)
    y = y * lp["pool_scale"].astype(f32)
    return y.astype(p.dtype), ext[:, ext.shape[1] - POOL_BUF:]


def mixer(h, lp, attend, pos0, h0_re, h0_im, conv_buf, pool_buf):
    B, T, _ = h.shape
    pa = split_cols(h @ lp["w_in"])
    q = pa[0].reshape(B, T, N_HEADS, HEAD_DIM)
    k = pa[1].reshape(B, T, N_HEADS, HEAD_DIM)
    v = pa[2].reshape(B, T, N_HEADS, HEAD_DIM)
    qi = pa[3].reshape(B, T, IDX_HEADS, IDX_DIM)
    ki = pa[4]
    wi = pa[5]
    y_att = attend(q, k, v, qi, ki, wi).reshape(B, T, D_GROUP)
    y_ssm, hr, hi = ssm_mixer(pa[6], h0_re, h0_im, lp)
    y_conv, new_conv = conv_mixer(pa[7], pa[8], conv_buf, lp)
    y_pool, new_pool = pool_mixer(pa[9], pool_buf, pos0, lp)
    out = jnp.concatenate([y_att, y_ssm.astype(h.dtype), y_conv, y_pool], axis=-1) @ lp["w_out"]
    return out, (k, v, ki, hr, hi, new_conv, new_pool)


def setup_inputs(seed: int = 0) -> dict:
    key = jax.random.key(seed)
    ks = iter(jax.random.split(key, 64))
    f32 = jnp.float32

    def nrm(shape, scale):
        return jax.random.normal(next(ks), shape, f32) * scale

    def gain(shape):
        return 1.0 + nrm(shape, 0.01)

    n_pages = PAST_LEN // PAGE_SIZE
    n_used = DEC_BATCH * n_pages
    n_pool = n_used + max(1, n_used // 4)
    page_table = jax.random.permutation(next(ks), n_pool)[:n_used].reshape(DEC_BATCH, n_pages).astype(jnp.int32)
    sshape = (DEPTH, SSM_GROUPS, SSM_STATE)
    return {
        "x_prompt": nrm((BATCH, SEQ, D_MODEL), 1.0),
        "x_sample": nrm((DEC_BATCH, DEC_SEQ, D_MODEL), 1.0),
        "cache_k": nrm((DEPTH, n_pool, PAGE_SIZE, N_HEADS, HEAD_DIM), 1.0),
        "cache_v": nrm((DEPTH, n_pool, PAGE_SIZE, N_HEADS, HEAD_DIM), 1.0),
        "cache_kidx": nrm((DEPTH, n_pool, PAGE_SIZE, IDX_DIM), 1.0),
        "page_table": page_table,
        "state_ssm_re": nrm((DEPTH, DEC_BATCH, SSM_GROUPS, SSM_STATE), 0.1),
        "state_ssm_im": nrm((DEPTH, DEC_BATCH, SSM_GROUPS, SSM_STATE), 0.1),
        "state_conv": nrm((DEPTH, DEC_BATCH, CONV_WIDTH - 1, CONV_CH), 0.5),
        "state_pool": nrm((DEPTH, DEC_BATCH, POOL_BUF, POOL_CH), 1.0),
        "meta_tokens": nrm((N_META, D_MODEL), 1.0),
        "rel_bias": nrm((N_BUCKETS, N_HEADS), 0.1),
        "norm_ffn1": gain((DEPTH, D_MODEL)),
        "ffn1_w_gate": nrm((DEPTH, D_MODEL, D_FF), D_MODEL ** -0.5),
        "ffn1_w_up": nrm((DEPTH, D_MODEL, D_FF), D_MODEL ** -0.5),
        "ffn1_w_down": nrm((DEPTH, D_FF, D_MODEL), D_FF ** -0.5),
        "norm_mix": gain((DEPTH, D_MODEL)),
        "w_in": nrm((DEPTH, D_MODEL, D_IN), D_MODEL ** -0.5),
        "w_out": nrm((DEPTH, D_MIX, D_MODEL), D_MIX ** -0.5),
        "ssm_a_re": -0.5 + nrm(sshape, 0.01),
        "ssm_a_im": math.pi * jnp.arange(SSM_STATE, dtype=f32) + nrm(sshape, 0.01),
        "ssm_log_dt": jax.random.uniform(next(ks), (DEPTH, SSM_GROUPS), f32, math.log(1e-3), math.log(1e-1)),
        "ssm_b_re": nrm((DEPTH, SSM_GROUPS, SSM_STATE, SSM_GROUP_CH), (2.0 * SSM_GROUP_CH) ** -0.5),
        "ssm_b_im": nrm((DEPTH, SSM_GROUPS, SSM_STATE, SSM_GROUP_CH), (2.0 * SSM_GROUP_CH) ** -0.5),
        "ssm_c_re": nrm((DEPTH, SSM_GROUPS, SSM_GROUP_CH, SSM_STATE), (2.0 * SSM_STATE) ** -0.5),
        "ssm_c_im": nrm((DEPTH, SSM_GROUPS, SSM_GROUP_CH, SSM_STATE), (2.0 * SSM_STATE) ** -0.5),
        "ssm_d": nrm((DEPTH, SSM_CH), 0.5),
        "ssm_w_glu": nrm((DEPTH, SSM_CH, SSM_CH), SSM_CH ** -0.5),
        "conv_w": nrm((DEPTH, CONV_WIDTH, CONV_CH), CONV_WIDTH ** -0.5),
        "conv_b": nrm((DEPTH, CONV_CH), 0.01),
        "conv_ln_g": gain((DEPTH, CONV_CH)),
        "conv_ln_b": nrm((DEPTH, CONV_CH), 0.01),
        "pool_w": nrm((DEPTH, len(POOL_WINDOWS), POOL_GROUP_CH, POOL_GROUP_CH), POOL_GROUP_CH ** -0.5),
        "pool_scale": gain((DEPTH, POOL_CH)),
        "norm_ffn2": gain((DEPTH, D_MODEL)),
        "ffn2_w_gate": nrm((DEPTH, D_MODEL, D_FF), D_MODEL ** -0.5),
        "ffn2_w_up": nrm((DEPTH, D_MODEL, D_FF), D_MODEL ** -0.5),
        "ffn2_w_down": nrm((DEPTH, D_FF, D_MODEL), D_FF ** -0.5),
        "norm_final": gain((D_MODEL,)),
    }


def reference(x_prompt, x_sample, cache_k, cache_v, cache_kidx, page_table,
              state_ssm_re, state_ssm_im, state_conv, state_pool,
              meta_tokens, rel_bias, norm_ffn1, ffn1_w_gate, ffn1_w_up, ffn1_w_down,
              norm_mix, w_in, w_out, ssm_a_re, ssm_a_im, ssm_log_dt, ssm_b_re, ssm_b_im,
              ssm_c_re, ssm_c_im, ssm_d, ssm_w_glu, conv_w, conv_b, conv_ln_g, conv_ln_b,
              pool_w, pool_scale, norm_ffn2, ffn2_w_gate, ffn2_w_up, ffn2_w_down, norm_final):
    B = x_prompt.shape[0]
    DB = x_sample.shape[0]
    past = page_table.shape[1] * PAGE_SIZE
    xp = jnp.concatenate([jnp.broadcast_to(meta_tokens[None].astype(x_prompt.dtype), (B, N_META, D_MODEL)), x_prompt], axis=1)
    xs = x_sample
    zr = jnp.zeros((B, SSM_GROUPS, SSM_STATE), state_ssm_re.dtype)
    zc = jnp.zeros((B, CONV_WIDTH - 1, CONV_CH), x_prompt.dtype)
    zp = jnp.zeros((B, POOL_BUF, POOL_CH), x_prompt.dtype)
    attend_p = functools.partial(attn_prompt, rel_bias=rel_bias)
    pst = [[] for _ in range(7)]
    sst = [[] for _ in range(7)]
    for l in range(DEPTH):
        lp = {"w_in": w_in[l], "w_out": w_out[l],
              "ssm_a_re": ssm_a_re[l], "ssm_a_im": ssm_a_im[l], "ssm_log_dt": ssm_log_dt[l],
              "ssm_b_re": ssm_b_re[l], "ssm_b_im": ssm_b_im[l], "ssm_c_re": ssm_c_re[l], "ssm_c_im": ssm_c_im[l],
              "ssm_d": ssm_d[l], "ssm_w_glu": ssm_w_glu[l],
              "conv_w": conv_w[l], "conv_b": conv_b[l], "conv_ln_g": conv_ln_g[l], "conv_ln_b": conv_ln_b[l],
              "pool_w": pool_w[l], "pool_scale": pool_scale[l]}
        attend_s = functools.partial(attn_sample, cache_k_l=cache_k[l], cache_v_l=cache_v[l],
                                     cache_ki_l=cache_kidx[l], page_table=page_table, rel_bias=rel_bias)
        xp = half_ffn(xp, norm_ffn1[l], ffn1_w_gate[l], ffn1_w_up[l], ffn1_w_down[l])
        xs = half_ffn(xs, norm_ffn1[l], ffn1_w_gate[l], ffn1_w_up[l], ffn1_w_down[l])
        out_p, st_p = mixer(rmsnorm(xp, norm_mix[l]), lp, attend_p, 0, zr, zr, zc, zp)
        out_s, st_s = mixer(rmsnorm(xs, norm_mix[l]), lp, attend_s, past,
                            state_ssm_re[l], state_ssm_im[l], state_conv[l], state_pool[l])
        xp = xp + out_p
        xs = xs + out_s
        xp = half_ffn(xp, norm_ffn2[l], ffn2_w_gate[l], ffn2_w_up[l], ffn2_w_down[l])
        xs = half_ffn(xs, norm_ffn2[l], ffn2_w_gate[l], ffn2_w_up[l], ffn2_w_down[l])
        for i in range(7):
            pst[i].append(st_p[i])
            sst[i].append(st_s[i])
    y_prompt = rmsnorm(xp, norm_final)[:, N_META:]
    y_sample = rmsnorm(xs, norm_final)
    p_k, p_v, p_kidx, p_ssm_re, p_ssm_im, p_conv, p_pool = [jnp.stack(a) for a in pst]
    s_k, s_v, s_kidx, s_ssm_re, s_ssm_im, s_conv, s_pool = [jnp.stack(a) for a in sst]
    return (y_prompt, y_sample, p_k, p_v, p_kidx, p_ssm_re, p_ssm_im, p_conv, p_pool,
            s_k, s_v, s_kidx, s_ssm_re, s_ssm_im, s_conv, s_pool)
```

```python
import functools
import math

import numpy as np
import jax
import jax.numpy as jnp
from jax import lax
from jax.experimental import pallas as pl
from jax.experimental.pallas import tpu as pltpu

F32 = jnp.float32
BF16 = jnp.bfloat16
I32 = jnp.int32

N_META = 16
N_HEADS = 4
HEAD_DIM = 64
IDX_HEADS = 4
IDX_DIM = 64
D_GROUP = 256
TOPK_MAX = 256
N_BUCKETS = 32
MAX_DISTANCE = 128
PAGE = 128
SSM_GROUPS = 16
SSM_GROUP_CH = 16
SSM_STATE = 64
SSM_LANES = SSM_GROUPS * SSM_STATE
CONV_WIDTH = 31
POOL_WINDOWS = (2, 4, 8, 16)
POOL_BUF = 15
EPS = 1e-6

BLK = 128
SUB = 8
NEG = -1e30
INT_MIN = -2 ** 31
KEY_NEG_INF = -2139095041
VMEM_LIMIT = 56 * 1024 * 1024

PROJ_COLS = 2432


def _dot(a, b):
    return jnp.dot(a, b, preferred_element_type=F32)


def _dot_nt(a, b):
    return lax.dot_general(a, b, (((1,), (1,)), ((), ())), preferred_element_type=F32)


def _rms_bf16(x, g):
    ms = jnp.mean(x * x, axis=-1, keepdims=True)
    return (x * lax.rsqrt(ms + EPS) * g).astype(BF16)


def _sigmoid(x):
    return 1.0 / (1.0 + jnp.exp(-x))


def _gelu_tanh(x):
    c = math.sqrt(2.0 / math.pi)
    return 0.5 * x * (1.0 + jnp.tanh(c * (x + 0.044715 * (x * x * x))))


def _params(sem):
    return pltpu.CompilerParams(dimension_semantics=sem, vmem_limit_bytes=VMEM_LIMIT)


def _ffn_kernel(x_ref, g_ref, wg_ref, wu_ref, wd_ref, o_ref, h_scr, acc_scr):
    j = pl.program_id(1)

    @pl.when(j == 0)
    def _():
        h_scr[...] = _rms_bf16(x_ref[...], g_ref[...])
        acc_scr[...] = jnp.zeros_like(acc_scr)

    h = h_scr[...]
    g = _dot(h, wg_ref[...])
    u = _dot(h, wu_ref[...])
    a = (g * _sigmoid(g) * u).astype(BF16)
    acc_scr[...] += _dot(a, wd_ref[...])

    @pl.when(j == pl.num_programs(1) - 1)
    def _():
        o_ref[...] = x_ref[...] + 0.5 * acc_scr[...]


def _ffn(x, g, wg, wu, wd, tm, tf=512):
    m, d = x.shape
    dff = wg.shape[1]
    return pl.pallas_call(
        _ffn_kernel,
        grid=(m // tm, dff // tf),
        in_specs=[
            pl.BlockSpec((tm, d), lambda i, j: (i, 0)),
            pl.BlockSpec((1, d), lambda i, j: (0, 0)),
            pl.BlockSpec((d, tf), lambda i, j: (0, j)),
            pl.BlockSpec((d, tf), lambda i, j: (0, j)),
            pl.BlockSpec((tf, d), lambda i, j: (j, 0)),
        ],
        out_specs=pl.BlockSpec((tm, d), lambda i, j: (i, 0)),
        out_shape=jax.ShapeDtypeStruct((m, d), F32),
        scratch_shapes=[pltpu.VMEM((tm, d), BF16), pltpu.VMEM((tm, d), F32)],
        compiler_params=_params(("parallel", "arbitrary")),
        name="half_ffn",
    )(x, g, wg, wu, wd)


def _proj_kernel(x_ref, g_ref, w_ref, k_ref, v_ref, ki_ref, wi_ref, qa_ref, qb_ref, kb_ref, vb_ref,
                 qib_ref, kia_ref, kib_ref, us_ref, ca_ref, cg_ref, pp_ref):
    h = _rms_bf16(x_ref[...], g_ref[...])
    z = _dot(h, w_ref[...])
    q = z[:, 0:256] * (HEAD_DIM ** -0.5)
    lane = lax.broadcasted_iota(I32, q.shape, 1)
    even = ((lane // HEAD_DIM) % 2) == 0
    qa_ref[...] = jnp.where(even, q, 0.0).astype(BF16)
    qb_ref[...] = jnp.where(even, 0.0, q).astype(BF16)
    k = z[:, 256:512]
    v = z[:, 512:768]
    k_ref[...] = k
    v_ref[...] = v
    kb_ref[...] = k.astype(BF16)
    vb_ref[...] = v.astype(BF16)
    qib_ref[...] = z[:, 768:1024].astype(BF16)
    kia = z[:, 1024:1152]
    ki_ref[...] = kia
    kia_ref[...] = kia.astype(BF16)
    kib_ref[...] = z[:, 1152:1280].astype(BF16)
    wi_ref[...] = z[:, 1280:1408]
    us_ref[...] = z[:, 1408:1664]
    ca_ref[...] = z[:, 1664:1920]
    cg_ref[...] = z[:, 1920:2176]
    pp_ref[...] = z[:, 2176:2432]


def _proj(x, g, w, tm):
    m, d = x.shape
    row = lambda i: (i, 0)
    f256 = jax.ShapeDtypeStruct((m, 256), F32)
    f128 = jax.ShapeDtypeStruct((m, 128), F32)
    b256 = jax.ShapeDtypeStruct((m, 256), BF16)
    b128 = jax.ShapeDtypeStruct((m, 128), BF16)
    shapes = [f256, f256, f128, f128, b256, b256, b256, b256, b256, b128, b128, f256, f256, f256, f256]
    return pl.pallas_call(
        _proj_kernel,
        grid=(m // tm,),
        in_specs=[
            pl.BlockSpec((tm, d), row),
            pl.BlockSpec((1, d), lambda i: (0, 0)),
            pl.BlockSpec((d, PROJ_COLS), lambda i: (0, 0)),
        ],
        out_specs=[pl.BlockSpec((tm, s.shape[1]), row) for s in shapes],
        out_shape=shapes,
        compiler_params=_params(("parallel",)),
        name="mixer_in_proj",
    )(x, g, w)


def _sortable_key(s):
    b = lax.bitcast_convert_type(s, I32)
    k = b ^ ((b >> 31) & 0x7FFFFFFF)
    return jnp.where(k == -1, 0, k)


def _lane_total(pc):
    ones = jnp.ones((BLK, BLK), BF16)
    return _dot(pc.astype(F32).astype(BF16), ones)


def _nth_largest_key(ks_ref, nkb, rows, n_sel):
    def bit_body(bi, t):
        c = t | jnp.left_shift(jnp.int32(1), 31 - bi)
        cs = c ^ INT_MIN

        def blk(j, pc):
            return pc + jnp.where(ks_ref[j] >= cs, 1, 0)

        pc = lax.fori_loop(0, nkb, blk, jnp.zeros((rows, BLK), I32))
        return jnp.where(_lane_total(pc) >= n_sel, c, t)

    t = lax.fori_loop(0, 32, bit_body, jnp.zeros((rows, BLK), I32))
    return t ^ INT_MIN


def _write_select_mask(ks_ref, am_ref, nkb, rows, n_sel):
    thr = _nth_largest_key(ks_ref, nkb, rows, n_sel)

    def cnt_blk(j, pc):
        return pc + jnp.where(ks_ref[j] > thr, 1, 0)

    n_gt = _lane_total(lax.fori_loop(0, nkb, cnt_blk, jnp.zeros((rows, BLK), I32)))
    need = n_sel - n_gt
    r = lax.broadcasted_iota(I32, (BLK, 2 * BLK), 0)
    c = lax.broadcasted_iota(I32, (BLK, 2 * BLK), 1)
    tri_ones = jnp.where((r <= c) | (c >= BLK), 1.0, 0.0).astype(BF16)

    def mask_blk(j, seen):
        kk = ks_ref[j]
        eq = kk == thr
        pr = _dot(jnp.where(eq, 1.0, 0.0).astype(BF16), tri_ones)
        rank = seen + pr[:, :BLK]
        tie = jnp.where(eq, jnp.where(rank <= need, 0.0, NEG), NEG)
        am = jnp.where(kk > thr, 0.0, tie)
        am_ref[j] = jnp.where(kk == KEY_NEG_INF, NEG, am)
        return seen + pr[:, BLK:]

    lax.fori_loop(0, nkb, mask_blk, jnp.zeros((rows, BLK), F32))


def _pattn_kernel(qa_ref, qb_ref, qib_ref, wi_ref, kb_ref, vb_ref, kia_ref, kib_ref, tbl_ref, o_ref,
                  ks_scr, am_scr, wb_scr, acc_scr, *, n_sel):
    i = pl.program_id(1)
    nkb = i + 1
    rows = BLK
    for h in range(IDX_HEADS):
        wb_scr[h] = jnp.broadcast_to(wi_ref[:, h:h + 1], (rows, BLK))
    q01 = qib_ref[:, 0:BLK]
    q23 = qib_ref[:, BLK:2 * BLK]
    rpos = i * BLK + lax.broadcasted_iota(I32, (rows, BLK), 0)
    cpos0 = lax.broadcasted_iota(I32, (rows, BLK), 1)

    def score_blk(j, carry):
        ksl = pl.ds(pl.multiple_of(j * BLK, BLK), BLK)
        ka = kia_ref[ksl, :]
        kb2 = kib_ref[ksl, :]
        s = wb_scr[0] * jnp.maximum(_dot_nt(q01, ka), 0.0)
        s = s + wb_scr[1] * jnp.maximum(_dot_nt(q01, kb2), 0.0)
        s = s + wb_scr[2] * jnp.maximum(_dot_nt(q23, ka), 0.0)
        s = s + wb_scr[3] * jnp.maximum(_dot_nt(q23, kb2), 0.0)
        s = jnp.where(j * BLK + cpos0 <= rpos, s, -jnp.inf)
        ks_scr[j] = _sortable_key(s)
        return carry

    lax.fori_loop(0, nkb, score_blk, 0)
    _write_select_mask(ks_scr, am_scr, nkb, rows, n_sel)

    acc_scr[...] = jnp.zeros_like(acc_scr)
    qh = (qa_ref[:, 0:BLK], qb_ref[:, 0:BLK], qa_ref[:, BLK:2 * BLK], qb_ref[:, BLK:2 * BLK])

    def attn_blk(j, carry):
        ms, ls = carry
        ksl = pl.ds(pl.multiple_of(j * BLK, BLK), BLK)
        kk = (kb_ref[ksl, 0:BLK], kb_ref[ksl, BLK:2 * BLK])
        vv = (vb_ref[ksl, 0:BLK], vb_ref[ksl, BLK:2 * BLK])
        am = am_scr[j]
        bsel = jnp.minimum(i - j, 2)
        new_m, new_l = [], []
        for h in range(N_HEADS):
            lg = _dot_nt(qh[h], kk[h // 2]) + tbl_ref[h, bsel] + am
            m_new = jnp.maximum(ms[h], jnp.max(lg, axis=-1, keepdims=True))
            alpha = jnp.exp(ms[h] - m_new)
            p = jnp.exp(lg - m_new)
            new_l.append(alpha * ls[h] + jnp.sum(p, axis=-1, keepdims=True))
            new_m.append(m_new)
            acc_scr[h] = alpha * acc_scr[h] + _dot(p.astype(BF16), vv[h // 2])
        return tuple(new_m), tuple(new_l)

    m0 = tuple(jnp.full((rows, 1), NEG, F32) for _ in range(N_HEADS))
    l0 = tuple(jnp.zeros((rows, 1), F32) for _ in range(N_HEADS))
    _, ls = lax.fori_loop(0, nkb, attn_blk, (m0, l0))
    lane = lax.broadcasted_iota(I32, (rows, BLK), 1)
    low = lane < HEAD_DIM
    o_ref[:, 0:BLK] = jnp.where(low, acc_scr[0] / ls[0], acc_scr[1] / ls[1])
    o_ref[:, BLK:2 * BLK] = jnp.where(low, acc_scr[2] / ls[2], acc_scr[3] / ls[3])


def _prompt_attention(pr, tbl, nb, lp, n_sel):
    nqb = lp // BLK
    m = nb * lp
    qblk = lambda c: pl.BlockSpec((BLK, c), lambda b, i: (b * nqb + i, 0))
    kres = lambda c: pl.BlockSpec((lp, c), lambda b, i: (b, 0))
    return pl.pallas_call(
        functools.partial(_pattn_kernel, n_sel=n_sel),
        grid=(nb, nqb),
        in_specs=[qblk(256), qblk(256), qblk(256), qblk(128), kres(256), kres(256), kres(128), kres(128),
                  pl.BlockSpec((N_HEADS, 3, BLK, BLK), lambda b, i: (0, 0, 0, 0))],
        out_specs=qblk(256),
        out_shape=jax.ShapeDtypeStruct((m, 256), F32),
        scratch_shapes=[pltpu.VMEM((nqb, BLK, BLK), I32), pltpu.VMEM((nqb, BLK, BLK), F32),
                        pltpu.VMEM((IDX_HEADS, BLK, BLK), F32), pltpu.VMEM((N_HEADS, BLK, BLK), F32)],
        compiler_params=_params(("parallel", "arbitrary")),
        name="prompt_sparse_attention",
    )(pr["qa"], pr["qb"], pr["qib"], pr["wi"], pr["kb"], pr["vb"], pr["kia"], pr["kib"], tbl)


def _ssm_tail(y_lin, u, vec_ref, wglu_ref):
    y = y_lin + vec_ref[0:1, :] * u
    z = _gelu_tanh(y)
    return z * _sigmoid(_dot(z.astype(BF16), wglu_ref[...]))


def _conv_tail(acc, vec_ref):
    y = acc + vec_ref[1:2, :]
    mu = jnp.mean(y, axis=-1, keepdims=True)
    var = jnp.mean(jnp.square(y - mu), axis=-1, keepdims=True)
    y = (y - mu) * lax.rsqrt(var + EPS) * vec_ref[2:3, :] + vec_ref[3:4, :]
    return y * _sigmoid(y)


def _pool_pick(sums, cnts, cur):
    lane = lax.broadcasted_iota(I32, cur.shape, 1)
    grp = lane // (D_GROUP // len(POOL_WINDOWS))
    d = sums[3] / cnts[3]
    for g in (2, 1, 0):
        d = jnp.where(grp == g, sums[g] / cnts[g], d)
    return d - cur


def _pmix_kernel(us_ref, ca_ref, cg_ref, pp_ref, bbr_ref, bbi_ref, cre_ref, cim_ref, coef_ref, vec_ref,
                 wglu_ref, cw_ref, pw_ref,
                 yssm_ref, yconv_ref, ypool_ref, u_ref, hfin_ref,
                 hr_scr, hi_scr, car_scr, cext_scr, pext_scr, *, fin_chunk, fin_grp):
    i = pl.program_id(1)
    rows = BLK

    @pl.when(i == 0)
    def _():
        car_scr[...] = jnp.zeros_like(car_scr)
        cext_scr[0:32, :] = jnp.zeros((32, D_GROUP), F32)
        pext_scr[0:16, :] = jnp.zeros((16, D_GROUP), F32)

    us = us_ref[...]
    ub = us.astype(BF16)
    hr_scr[...] = _dot(ub, bbr_ref[...])
    hi_scr[...] = _dot(ub, bbi_ref[...])
    k1r, k1i, k2r, k2i, k4r, k4i, pwr, pwi = (coef_ref[n] for n in range(8))

    def grp_body(r, carry):
        cr, ci = carry
        sl = pl.ds(pl.multiple_of(r * SUB, SUB), SUB)
        xr = hr_scr[sl, :]
        xi = hi_scr[sl, :]
        for sh, (ar, ai) in ((1, (k1r, k1i)), (2, (k2r, k2i)), (4, (k4r, k4i))):
            sr = pltpu.roll(xr, sh, 0)
            si = pltpu.roll(xi, sh, 0)
            xr, xi = xr + ar * sr - ai * si, xi + ar * si + ai * sr
        hr = xr + pwr * cr - pwi * ci
        hi = xi + pwr * ci + pwi * cr
        hr_scr[sl, :] = hr
        hi_scr[sl, :] = hi
        return (jnp.broadcast_to(hr[SUB - 1:SUB, :], hr.shape), jnp.broadcast_to(hi[SUB - 1:SUB, :], hi.shape))

    cr, ci = lax.fori_loop(0, rows // SUB, grp_body, (car_scr[0], car_scr[1]))
    car_scr[0] = cr
    car_scr[1] = ci

    @pl.when(i == fin_chunk)
    def _():
        hfin_ref[0] = hr_scr[fin_grp * SUB:(fin_grp + 1) * SUB, :]
        hfin_ref[1] = hi_scr[fin_grp * SUB:(fin_grp + 1) * SUB, :]

    y_lin = _dot(hr_scr[...].astype(BF16), cre_ref[...]) - _dot(hi_scr[...].astype(BF16), cim_ref[...])
    yssm_ref[...] = _ssm_tail(y_lin, us, vec_ref, wglu_ref)

    u = ca_ref[...] * _sigmoid(cg_ref[...])
    u_ref[...] = u
    cext_scr[32:32 + rows, :] = u
    acc = jnp.zeros((rows, D_GROUP), F32)
    for k in range(CONV_WIDTH):
        acc = acc + cw_ref[k:k + 1, :] * cext_scr[2 + k:2 + k + rows, :]
    yconv_ref[...] = _conv_tail(acc, vec_ref)
    cext_scr[0:32, :] = cext_scr[rows:rows + 32, :]

    p = pp_ref[...]
    pext_scr[16:16 + rows, :] = p
    e = pext_scr[...]
    s2 = e + pltpu.roll(e, 1, 0)
    s4 = s2 + pltpu.roll(s2, 2, 0)
    s8 = s4 + pltpu.roll(s4, 4, 0)
    s16 = s8 + pltpu.roll(s8, 8, 0)
    pos1 = (i * rows + 1 + lax.broadcasted_iota(I32, (rows, D_GROUP), 0)).astype(F32)
    sums = [s[16:16 + rows, :] for s in (s2, s4, s8, s16)]
    cnts = [jnp.minimum(pos1, float(w)) for w in POOL_WINDOWS]
    d = _pool_pick(sums, cnts, p)
    ypool_ref[...] = _dot(d.astype(BF16), pw_ref[...]) * vec_ref[4:5, :]
    pext_scr[0:16, :] = pext_scr[rows:rows + 16, :]


def _prompt_mixers(pr, sp, nb, lp, t_real):
    nqb = lp // BLK
    m = nb * lp
    blk = pl.BlockSpec((BLK, D_GROUP), lambda b, i: (b * nqb + i, 0))
    full = lambda a: pl.BlockSpec(a.shape, lambda b, i: (0,) * a.ndim)
    last = t_real - 1
    weights = [sp["bbr"], sp["bbi"], sp["cre"], sp["cim"], sp["coef"], sp["vec"], sp["wglu"], sp["cw"], sp["pw"]]
    f256 = jax.ShapeDtypeStruct((m, D_GROUP), F32)
    return pl.pallas_call(
        functools.partial(_pmix_kernel, fin_chunk=last // BLK, fin_grp=(last % BLK) // SUB),
        grid=(nb, nqb),
        in_specs=[blk, blk, blk, blk] + [full(w) for w in weights],
        out_specs=[blk, blk, blk, blk, pl.BlockSpec((None, 2, SUB, SSM_LANES), lambda b, i: (b, 0, 0, 0))],
        out_shape=[f256, f256, f256, f256, jax.ShapeDtypeStruct((nb, 2, SUB, SSM_LANES), F32)],
        scratch_shapes=[pltpu.VMEM((BLK, SSM_LANES), F32), pltpu.VMEM((BLK, SSM_LANES), F32),
                        pltpu.VMEM((2, SUB, SSM_LANES), F32), pltpu.VMEM((32 + BLK, D_GROUP), F32),
                        pltpu.VMEM((16 + BLK, D_GROUP), F32)],
        compiler_params=_params(("parallel", "arbitrary")),
        name="prompt_ssm_conv_pool",
    )(pr["us"], pr["ca"], pr["cg"], pr["pp"], *weights)


def _smix_kernel(us_ref, ca_ref, cg_ref, pp_ref, h0r_ref, h0i_ref, cst_ref, pst_ref,
                 bbr_ref, bbi_ref, cre_ref, cim_ref, ab_ref, vec_ref, wglu_ref, cw_ref, pw_ref,
                 yssm_ref, yconv_ref, ypool_ref, u_ref, hr_ref, hi_ref, *, t_new):
    c = D_GROUP
    sl = lambda t: slice(t * c, (t + 1) * c)
    ar = ab_ref[0:1, :]
    ai = ab_ref[1:2, :]
    hr = h0r_ref[...]
    hi = h0i_ref[...]
    for t in range(t_new):
        us = us_ref[:, sl(t)]
        ub = us.astype(BF16)
        hr, hi = ar * hr - ai * hi + _dot(ub, bbr_ref[...]), ar * hi + ai * hr + _dot(ub, bbi_ref[...])
        y_lin = _dot(hr.astype(BF16), cre_ref[...]) - _dot(hi.astype(BF16), cim_ref[...])
        yssm_ref[:, sl(t)] = _ssm_tail(y_lin, us, vec_ref, wglu_ref)
    hr_ref[...] = hr
    hi_ref[...] = hi

    n_cst = CONV_WIDTH - 1
    ext = [cst_ref[:, sl(j)] for j in range(n_cst)]
    for t in range(t_new):
        u = ca_ref[:, sl(t)] * _sigmoid(cg_ref[:, sl(t)])
        u_ref[:, sl(t)] = u
        ext.append(u)
    for t in range(t_new):
        acc = cw_ref[0:1, :] * ext[t]
        for k in range(1, CONV_WIDTH):
            acc = acc + cw_ref[k:k + 1, :] * ext[t + k]
        yconv_ref[:, sl(t)] = _conv_tail(acc, vec_ref)

    pe = [pst_ref[:, sl(j)] for j in range(POOL_BUF)] + [pp_ref[:, sl(t)] for t in range(t_new)]
    n = len(pe)
    s2 = [None] + [pe[j] + pe[j - 1] for j in range(1, n)]
    s4 = [None] * 3 + [s2[j] + s2[j - 2] for j in range(3, n)]
    s8 = [None] * 7 + [s4[j] + s4[j - 4] for j in range(7, n)]
    s16 = [None] * 15 + [s8[j] + s8[j - 8] for j in range(15, n)]
    for t in range(t_new):
        j = POOL_BUF + t
        cnts = [float(w) for w in POOL_WINDOWS]
        d = _pool_pick([s2[j], s4[j], s8[j], s16[j]], cnts, pe[j])
        ypool_ref[:, sl(t)] = _dot(d.astype(BF16), pw_ref[...]) * vec_ref[4:5, :]


def _sample_mixers(sm, sp, h0r, h0i, cst, pst, nseq, t_new):
    wide = lambda a: a.reshape(nseq, t_new * D_GROUP)
    args = [wide(sm["us"]), wide(sm["ca"]), wide(sm["cg"]), wide(sm["pp"]), h0r, h0i, cst, pst,
            sp["bbr"], sp["bbi"], sp["cre"], sp["cim"], sp["ab"], sp["vec"], sp["wglu"], sp["cw"], sp["pw"]]
    full = lambda a: pl.BlockSpec(a.shape, lambda i: (0,) * a.ndim)
    fw = jax.ShapeDtypeStruct((nseq, t_new * D_GROUP), F32)
    fs = jax.ShapeDtypeStruct((nseq, SSM_LANES), F32)
    shapes = [fw, fw, fw, fw, fs, fs]
    return pl.pallas_call(
        functools.partial(_smix_kernel, t_new=t_new),
        grid=(1,),
        in_specs=[full(a) for a in args],
        out_specs=[full(s) for s in shapes],
        out_shape=shapes,
        compiler_params=_params(("arbitrary",)),
        name="sample_ssm_conv_pool",
    )(*args)


def _sscore_kernel(pt_ref, qib_ref, wi_ref, kin_ref, *rest, n_pages, t_new):
    page_refs = rest[:n_pages]
    s_ref = rest[n_pages]
    qif = qib_ref[...].astype(F32)
    qst = jnp.concatenate([qif[:, h * IDX_DIM:(h + 1) * IDX_DIM] for h in range(IDX_HEADS)], axis=0).astype(BF16)
    wcol = jnp.concatenate([wi_ref[:, h:h + 1] for h in range(IDX_HEADS)], axis=0)

    def score(keys_bf16):
        d = wcol * jnp.maximum(_dot_nt(qst, keys_bf16), 0.0)
        s = d[0:t_new]
        for h in range(1, IDX_HEADS):
            s = s + d[h * t_new:(h + 1) * t_new]
        return s

    for p in range(n_pages):
        s_ref[:, p * PAGE:(p + 1) * PAGE] = score(page_refs[p][...].astype(BF16))
    knew = jnp.concatenate([kin_ref[...].astype(F32)[:, 0:IDX_DIM], jnp.zeros((PAGE - t_new, IDX_DIM), F32)], axis=0)
    s = score(knew.astype(BF16))
    r = lax.broadcasted_iota(I32, s.shape, 0)
    c = lax.broadcasted_iota(I32, s.shape, 1)
    s_ref[:, n_pages * PAGE:(n_pages + 1) * PAGE] = jnp.where(c <= r, s, -jnp.inf)


def _sample_scores(sm, cache_kidx, pt_flat, layer, nseq, t_new, n_pages):
    ncol = (n_pages + 1) * PAGE
    row = lambda c: pl.BlockSpec((t_new, c), lambda b, pt: (b, 0))
    page = lambda p: pl.BlockSpec((None, None, PAGE, IDX_DIM),
                                  lambda b, pt, p=p: (layer, pt[b * n_pages + p], 0, 0))
    grid_spec = pltpu.PrefetchScalarGridSpec(
        num_scalar_prefetch=1, grid=(nseq,),
        in_specs=[row(256), row(128), row(128)] + [page(p) for p in range(n_pages)],
        out_specs=pl.BlockSpec((None, t_new, ncol), lambda b, pt: (b, 0, 0)))
    return pl.pallas_call(
        functools.partial(_sscore_kernel, n_pages=n_pages, t_new=t_new),
        grid_spec=grid_spec,
        out_shape=jax.ShapeDtypeStruct((nseq, t_new, ncol), F32),
        compiler_params=_params(("parallel",)),
        name="sample_indexer_scores",
    )(pt_flat, sm["qib"], sm["wi"], sm["kia"], *([cache_kidx] * n_pages))


def _sselect_kernel(s_ref, am_ref, ks_scr, am_scr, *, ncb, n_sel):
    for j in range(ncb):
        ks_scr[j] = _sortable_key(s_ref[:, j * BLK:(j + 1) * BLK])
    _write_select_mask(ks_scr, am_scr, ncb, BLK, n_sel)
    for j in range(ncb):
        am_ref[:, j * BLK:(j + 1) * BLK] = am_scr[j]


def _sample_select(scores, n_sel):
    m, ncol = scores.shape
    ncb = ncol // BLK
    spec = pl.BlockSpec((BLK, ncol), lambda i: (i, 0))
    return pl.pallas_call(
        functools.partial(_sselect_kernel, ncb=ncb, n_sel=n_sel),
        grid=(m // BLK,),
        in_specs=[spec],
        out_specs=spec,
        out_shape=jax.ShapeDtypeStruct((m, ncol), F32),
        scratch_shapes=[pltpu.VMEM((ncb, BLK, BLK), I32), pltpu.VMEM((ncb, BLK, BLK), F32)],
        compiler_params=_params(("parallel",)),
        name="sample_topn_mask",
    )(scores)


def _sattn_kernel(pt_ref, q_ref, am_ref, kn_ref, vn_ref, tbl_ref, *rest, n_pages, t_new):
    k_refs = rest[:n_pages]
    v_refs = rest[n_pages:2 * n_pages]
    o_ref = rest[2 * n_pages]
    q = q_ref[...].astype(F32)
    lane = lax.broadcasted_iota(I32, q.shape, 1)
    qbd = jnp.concatenate([jnp.where(lane // HEAD_DIM == h, q, 0.0) for h in range(N_HEADS)], axis=0).astype(BF16)
    zpad = jnp.zeros((PAGE - t_new, D_GROUP), F32)
    pad_page = lambda r: jnp.concatenate([r[...].astype(F32), zpad], axis=0).astype(BF16)
    keys = [k_refs[p][...].astype(BF16) for p in range(n_pages)] + [pad_page(kn_ref)]
    vals = [v_refs[p][...].astype(BF16) for p in range(n_pages)] + [pad_page(vn_ref)]
    lgs = []
    for p in range(n_pages + 1):
        am = am_ref[:, p * PAGE:(p + 1) * PAGE]
        am4 = jnp.concatenate([am] * N_HEADS, axis=0)
        tsel = 0 if p < n_pages - 1 else (1 if p == n_pages - 1 else 2)
        lgs.append(_dot_nt(qbd, keys[p]) + tbl_ref[tsel] + am4)
    mx = lgs[0]
    for lg in lgs[1:]:
        mx = jnp.maximum(mx, lg)
    mx = jnp.max(mx, axis=-1, keepdims=True)
    ps = [jnp.exp(lg - mx) for lg in lgs]
    tot = ps[0]
    for pp_ in ps[1:]:
        tot = tot + pp_
    den = jnp.sum(tot, axis=-1, keepdims=True)
    acc = _dot(ps[0].astype(BF16), vals[0])
    for p in range(1, n_pages + 1):
        acc = acc + _dot(ps[p].astype(BF16), vals[p])
    acc = acc / den
    out = acc[0:t_new]
    for h in range(1, N_HEADS):
        out = jnp.where(lane // HEAD_DIM == h, acc[h * t_new:(h + 1) * t_new], out)
    o_ref[...] = out


def _sample_attention(sm, qs, am, tbl, cache_k, cache_v, pt_flat, layer, nseq, t_new, n_pages):
    ncol = (n_pages + 1) * PAGE
    row = lambda c: pl.BlockSpec((t_new, c), lambda b, pt: (b, 0))
    page = lambda p: pl.BlockSpec((None, None, PAGE, D_GROUP),
                                  lambda b, pt, p=p: (layer, pt[b * n_pages + p], 0, 0))
    grid_spec = pltpu.PrefetchScalarGridSpec(
        num_scalar_prefetch=1, grid=(nseq,),
        in_specs=[row(256), row(ncol), row(256), row(256),
                  pl.BlockSpec(tbl.shape, lambda b, pt: (0, 0, 0))]
        + [page(p) for p in range(n_pages)] + [page(p) for p in range(n_pages)],
        out_specs=row(256))
    return pl.pallas_call(
        functools.partial(_sattn_kernel, n_pages=n_pages, t_new=t_new),
        grid_spec=grid_spec,
        out_shape=jax.ShapeDtypeStruct((nseq * t_new, D_GROUP), F32),
        compiler_params=_params(("parallel",)),
        name="sample_sparse_attention",
    )(pt_flat, qs, am, sm["kb"], sm["vb"], tbl, *([cache_k] * n_pages), *([cache_v] * n_pages))


def _outproj_kernel(x_ref, a_ref, s_ref, c_ref, p_ref, w_ref, o_ref):
    acc = x_ref[...]
    for n, r in enumerate((a_ref, s_ref, c_ref, p_ref)):
        acc = acc + _dot(r[...].astype(BF16), w_ref[n * D_GROUP:(n + 1) * D_GROUP, :])
    o_ref[...] = acc


def _outproj(x, ys, w, tm):
    m, d = x.shape
    row = lambda c: pl.BlockSpec((tm, c), lambda i: (i, 0))
    return pl.pallas_call(
        _outproj_kernel,
        grid=(m // tm,),
        in_specs=[row(d)] + [row(D_GROUP)] * 4 + [pl.BlockSpec(w.shape, lambda i: (0, 0))],
        out_specs=row(d),
        out_shape=jax.ShapeDtypeStruct((m, d), F32),
        compiler_params=_params(("parallel",)),
        name="mixer_out_proj",
    )(x, *ys, w)


def _rmsnorm_kernel(x_ref, g_ref, o_ref):
    x = x_ref[...]
    ms = jnp.mean(x * x, axis=-1, keepdims=True)
    o_ref[...] = x * lax.rsqrt(ms + EPS) * g_ref[...]


def _rmsnorm(x, g, tm):
    m, d = x.shape
    return pl.pallas_call(
        _rmsnorm_kernel,
        grid=(m // tm,),
        in_specs=[pl.BlockSpec((tm, d), lambda i: (i, 0)), pl.BlockSpec((1, d), lambda i: (0, 0))],
        out_specs=pl.BlockSpec((tm, d), lambda i: (i, 0)),
        out_shape=jax.ShapeDtypeStruct((m, d), F32),
        compiler_params=_params(("parallel",)),
        name="final_rmsnorm",
    )(x, g)


def _bucket_table(dist):
    n = np.maximum(dist, 0)
    max_exact = N_BUCKETS // 2
    nf = np.maximum(n, 1).astype(np.float32)
    large = max_exact + (np.log(nf / np.float32(max_exact)) / np.float32(math.log(MAX_DISTANCE / max_exact))
                         * np.float32(N_BUCKETS - max_exact)).astype(np.int32)
    large = np.minimum(large, N_BUCKETS - 1)
    return np.where(n < max_exact, n, large).astype(np.int32)


def _bias_tables(rel_bias, t_new):
    r = np.arange(BLK)[:, None]
    c = np.arange(BLK)[None, :]
    far = np.full((BLK, BLK), 2 * BLK)
    pidx = np.stack([_bucket_table(r - c), _bucket_table(BLK + r - c), _bucket_table(far)])
    ptbl = jnp.transpose(rel_bias[pidx], (3, 0, 1, 2))
    t = np.arange(t_new)[:, None]
    sidx = np.stack([_bucket_table(np.full((t_new, BLK), 2 * BLK)), _bucket_table(BLK + t - c),
                     _bucket_table(t - c)])
    stbl = jnp.transpose(rel_bias[sidx], (0, 3, 1, 2)).reshape(3, N_HEADS * t_new, BLK)
    return ptbl.astype(F32), stbl.astype(F32)


def _cmul(a, b):
    return a[0] * b[0] - a[1] * b[1], a[0] * b[1] + a[1] * b[0]


def _ssm_params(a_re, a_im, log_dt, b_re, b_im, c_re, c_im):
    nl = a_re.shape[0]
    dt = jnp.exp(log_dt)[..., None]
    decay = jnp.exp(dt * a_re)
    ab_re = decay * jnp.cos(dt * a_im)
    ab_im = decay * jnp.sin(dt * a_im)
    den = a_re * a_re + a_im * a_im
    f_re = ((ab_re - 1.0) * a_re + ab_im * a_im) / den
    f_im = (ab_im * a_re - (ab_re - 1.0) * a_im) / den
    bb_re = f_re[..., None] * b_re - f_im[..., None] * b_im
    bb_im = f_re[..., None] * b_im + f_im[..., None] * b_re
    eye = jnp.eye(SSM_GROUPS, dtype=F32)
    pack_b = lambda b: jnp.einsum("lgpc,gh->lgchp", b, eye).reshape(nl, D_GROUP, SSM_LANES).astype(BF16)
    pack_c = lambda c: jnp.einsum("lgcp,gh->lgphc", c, eye).reshape(nl, SSM_LANES, D_GROUP).astype(BF16)
    a1 = (ab_re.reshape(nl, SSM_LANES), ab_im.reshape(nl, SSM_LANES))
    a2 = _cmul(a1, a1)
    a3 = _cmul(a2, a1)
    a4 = _cmul(a2, a2)
    a5 = _cmul(a4, a1)
    a6 = _cmul(a4, a2)
    a7 = _cmul(a4, a3)
    a8 = _cmul(a4, a4)
    t = jnp.arange(SUB)[None, :, None]
    gate = lambda a, s: jnp.where(t >= s, a[:, None, :], 0.0)
    pw_re = jnp.stack([a[0] for a in (a1, a2, a3, a4, a5, a6, a7, a8)], axis=1)
    pw_im = jnp.stack([a[1] for a in (a1, a2, a3, a4, a5, a6, a7, a8)], axis=1)
    coef = jnp.stack([gate(a1[0], 1), gate(a1[1], 1), gate(a2[0], 2), gate(a2[1], 2),
                      gate(a4[0], 4), gate(a4[1], 4), pw_re, pw_im], axis=1)
    ab = jnp.stack([a1[0], a1[1]], axis=1)
    return pack_b(bb_re), pack_b(bb_im), pack_c(c_re), pack_c(c_im), coef, ab


def _pad_rows(a, n):
    return jnp.concatenate([a, jnp.zeros((n - a.shape[0],) + a.shape[1:], a.dtype)], axis=0)


def kernel(x_prompt, x_sample, cache_k, cache_v, cache_kidx, page_table, state_ssm_re, state_ssm_im, state_conv, state_pool, meta_tokens, rel_bias, norm_ffn1, ffn1_w_gate, ffn1_w_up, ffn1_w_down, norm_mix, w_in, w_out, ssm_a_re, ssm_a_im, ssm_log_dt, ssm_b_re, ssm_b_im, ssm_c_re, ssm_c_im, ssm_d, ssm_w_glu, conv_w, conv_b, conv_ln_g, conv_ln_b, pool_w, pool_scale, norm_ffn2, ffn2_w_gate, ffn2_w_up, ffn2_w_down, norm_final):
    nb, seq, d = x_prompt.shape
    nseq, t_new, _ = x_sample.shape
    depth = w_in.shape[0]
    n_pages = page_table.shape[1]
    n_pool = cache_k.shape[1]
    t_real = seq + N_META
    lp = -(-t_real // BLK) * BLK
    past = n_pages * PAGE
    n_sel_p = min(TOPK_MAX, t_real // 4)
    n_sel_s = min(TOPK_MAX, (past + t_new) // 4)
    mp, ms = nb * lp, nseq * t_new
    tm_p, tm_s = 768, 512
    assert mp % tm_p == 0 and ms % tm_s == 0 and t_new == SUB

    meta = jnp.broadcast_to(meta_tokens[None].astype(F32), (nb, N_META, d))
    xp = jnp.concatenate([meta, x_prompt, jnp.zeros((nb, lp - t_real, d), F32)], axis=1).reshape(mp, d)
    xs = x_sample.reshape(ms, d)

    zc = lambda n: jnp.zeros((depth, d, n), F32)
    q_, k_, v_, qi_, ki_, wi_, rest = (w_in[..., 0:256], w_in[..., 256:512], w_in[..., 512:768], w_in[..., 768:1024],
                                       w_in[..., 1024:1088], w_in[..., 1088:1092], w_in[..., 1092:])
    w_in_p = jnp.concatenate([q_, k_, v_, qi_, ki_, zc(64), zc(64), ki_, wi_, zc(124), rest], axis=-1).astype(BF16)
    w_out_b = w_out.astype(BF16)
    ffn_w = [[w.astype(BF16) for w in ws] for ws in ((ffn1_w_gate, ffn1_w_up, ffn1_w_down),
                                                     (ffn2_w_gate, ffn2_w_up, ffn2_w_down))]
    bbr, bbi, cre, cim, coef, ab = _ssm_params(ssm_a_re, ssm_a_im, ssm_log_dt, ssm_b_re, ssm_b_im, ssm_c_re, ssm_c_im)
    zrow = jnp.zeros((depth, D_GROUP), F32)
    vec = jnp.stack([ssm_d, conv_b, conv_ln_g, conv_ln_b, pool_scale, zrow, zrow, zrow], axis=1)
    cw = jnp.concatenate([conv_w, jnp.zeros((depth, 1, D_GROUP), F32)], axis=1)
    ng = len(POOL_WINDOWS)
    pw = jnp.einsum("lgcd,gh->lgchd", pool_w, jnp.eye(ng, dtype=F32)).reshape(depth, D_GROUP, D_GROUP).astype(BF16)
    wglu = ssm_w_glu.astype(BF16)
    ptbl, stbl = _bias_tables(rel_bias, t_new)

    ck = cache_k.reshape(depth, n_pool, PAGE, D_GROUP)
    cv = cache_v.reshape(depth, n_pool, PAGE, D_GROUP)
    pt_flat = page_table.reshape(-1).astype(I32)
    h0r = state_ssm_re.reshape(depth, nseq, SSM_LANES)
    h0i = state_ssm_im.reshape(depth, nseq, SSM_LANES)
    cst = state_conv.reshape(depth, nseq, (CONV_WIDTH - 1) * D_GROUP)
    pst = state_pool.reshape(depth, nseq, POOL_BUF * D_GROUP)
    names = ("k", "v", "ki", "wi", "qa", "qb", "kb", "vb", "qib", "kia", "kib", "us", "ca", "cg", "pp")

    pst_out = [[] for _ in range(7)]
    sst_out = [[] for _ in range(7)]
    for l in range(depth):
        g1 = norm_ffn1[l][None]
        xp = _ffn(xp, g1, ffn_w[0][0][l], ffn_w[0][1][l], ffn_w[0][2][l], tm_p)
        xs = _ffn(xs, g1, ffn_w[0][0][l], ffn_w[0][1][l], ffn_w[0][2][l], tm_s)

        gm = norm_mix[l][None]
        pr = dict(zip(names, _proj(xp, gm, w_in_p[l], tm_p // 2)))
        sm = dict(zip(names, _proj(xs, gm, w_in_p[l], tm_s)))
        sp = dict(bbr=bbr[l], bbi=bbi[l], cre=cre[l], cim=cim[l], coef=coef[l], ab=ab[l], vec=vec[l],
                  wglu=wglu[l], cw=cw[l], pw=pw[l])

        ya_p = _prompt_attention(pr, ptbl, nb, lp, n_sel_p)
        ys_p, yc_p, yp_p, u_p, hfin = _prompt_mixers(pr, sp, nb, lp, t_real)
        xp = _outproj(xp, (ya_p, ys_p, yc_p, yp_p), w_out_b[l], tm_p)

        scores = _sample_scores(sm, cache_kidx, pt_flat, l, nseq, t_new, n_pages)
        am = _sample_select(scores.reshape(ms, -1), n_sel_s)
        qs = sm["qa"] + sm["qb"]
        ya_s = _sample_attention(sm, qs, am, stbl, ck, cv, pt_flat, l, nseq, t_new, n_pages)
        ys_s, yc_s, yp_s, u_s, hr_s, hi_s = _sample_mixers(sm, sp, h0r[l], h0i[l], cst[l], pst[l], nseq, t_new)
        wide = lambda a: a.reshape(ms, D_GROUP)
        xs = _outproj(xs, (ya_s, wide(ys_s), wide(yc_s), wide(yp_s)), w_out_b[l], tm_s)

        g2 = norm_ffn2[l][None]
        xp = _ffn(xp, g2, ffn_w[1][0][l], ffn_w[1][1][l], ffn_w[1][2][l], tm_p)
        xs = _ffn(xs, g2, ffn_w[1][0][l], ffn_w[1][1][l], ffn_w[1][2][l], tm_s)

        seq3 = lambda a: a.reshape(nb, lp, -1)[:, :t_real]
        fin_row = (t_real - 1) % SUB
        pst_out[0].append(seq3(pr["k"]).reshape(nb, t_real, N_HEADS, HEAD_DIM))
        pst_out[1].append(seq3(pr["v"]).reshape(nb, t_real, N_HEADS, HEAD_DIM))
        pst_out[2].append(seq3(pr["ki"])[..., :IDX_DIM])
        pst_out[3].append(hfin[:, 0, fin_row].reshape(nb, SSM_GROUPS, SSM_STATE))
        pst_out[4].append(hfin[:, 1, fin_row].reshape(nb, SSM_GROUPS, SSM_STATE))
        pst_out[5].append(seq3(u_p)[:, t_real - (CONV_WIDTH - 1):])
        pst_out[6].append(seq3(pr["pp"])[:, t_real - POOL_BUF:])
        sst_out[0].append(sm["k"].reshape(nseq, t_new, N_HEADS, HEAD_DIM))
        sst_out[1].append(sm["v"].reshape(nseq, t_new, N_HEADS, HEAD_DIM))
        sst_out[2].append(sm["ki"][:, :IDX_DIM].reshape(nseq, t_new, IDX_DIM))
        sst_out[3].append(hr_s.reshape(nseq, SSM_GROUPS, SSM_STATE))
        sst_out[4].append(hi_s.reshape(nseq, SSM_GROUPS, SSM_STATE))
        sst_out[5].append(jnp.concatenate([state_conv[l][:, t_new:], u_s.reshape(nseq, t_new, D_GROUP)], axis=1))
        sst_out[6].append(jnp.concatenate([state_pool[l][:, t_new:], sm["pp"].reshape(nseq, t_new, D_GROUP)], axis=1))

    gf = norm_final[None]
    y_prompt = _rmsnorm(xp, gf, tm_p).reshape(nb, lp, d)[:, N_META:t_real]
    y_sample = _rmsnorm(xs, gf, tm_s).reshape(nseq, t_new, d)
    return (y_prompt, y_sample, *[jnp.stack(a) for a in pst_out], *[jnp.stack(a) for a in sst_out])
```

```python
import functools
import math

import numpy as np
import jax
import jax.numpy as jnp
from jax import lax
from jax.experimental import pallas as pl
from jax.experimental.pallas import tpu as pltpu

F32 = jnp.float32
BF16 = jnp.bfloat16
I32 = jnp.int32
I16 = jnp.int16

N_META = 16
N_HEADS = 4
HEAD_DIM = 64
IDX_HEADS = 4
IDX_DIM = 64
D_GROUP = 256
TOPK_MAX = 256
N_BUCKETS = 32
MAX_DISTANCE = 128
PAGE = 128
SSM_GROUPS = 16
SSM_GROUP_CH = 16
SSM_STATE = 64
SSM_LANES = SSM_GROUPS * SSM_STATE
CONV_WIDTH = 31
POOL_WINDOWS = (2, 4, 8, 16)
POOL_BUF = 15
EPS = 1e-6

BLK = 128
SUB = 8
QB = 256
KB = 256
KPQ = QB // KB
HALF_ROWS = 16
HALF_BIAS = 32768
NEG = -1e30
INT_MIN = -2 ** 31
KEY_NEG_INF = -2139095041
VMEM_LIMIT = 56 * 1024 * 1024

PROJ_COLS = 2432


def _dot(a, b):
    return jnp.dot(a, b, preferred_element_type=F32)


def _dot_nt(a, b):
    return lax.dot_general(a, b, (((1,), (1,)), ((), ())), preferred_element_type=F32)


def _rms_bf16(x, g):
    ms = jnp.mean(x * x, axis=-1, keepdims=True)
    return (x * lax.rsqrt(ms + EPS) * g).astype(BF16)


def _sigmoid(x):
    return 1.0 / (1.0 + jnp.exp(-x))


def _gelu_tanh(x):
    c = math.sqrt(2.0 / math.pi)
    return 0.5 * x * (1.0 + jnp.tanh(c * (x + 0.044715 * (x * x * x))))


def _params(sem):
    return pltpu.CompilerParams(dimension_semantics=sem, vmem_limit_bytes=VMEM_LIMIT)


def _ffn_kernel(x_ref, g_ref, wg_ref, wu_ref, wd_ref, o_ref, h_scr, acc_scr):
    j = pl.program_id(1)

    @pl.when(j == 0)
    def _():
        h_scr[...] = _rms_bf16(x_ref[...], g_ref[...])
        acc_scr[...] = jnp.zeros_like(acc_scr)

    h = h_scr[...]
    g = _dot(h, wg_ref[...])
    u = _dot(h, wu_ref[...])
    a = (g * _sigmoid(g) * u).astype(BF16)
    acc_scr[...] += _dot(a, wd_ref[...])

    @pl.when(j == pl.num_programs(1) - 1)
    def _():
        o_ref[...] = x_ref[...] + 0.5 * acc_scr[...]


def _ffn(x, g, wg, wu, wd, tm, tf=512):
    m, d = x.shape
    dff = wg.shape[1]
    return pl.pallas_call(
        _ffn_kernel,
        grid=(m // tm, dff // tf),
        in_specs=[
            pl.BlockSpec((tm, d), lambda i, j: (i, 0)),
            pl.BlockSpec((1, d), lambda i, j: (0, 0)),
            pl.BlockSpec((d, tf), lambda i, j: (0, j)),
            pl.BlockSpec((d, tf), lambda i, j: (0, j)),
            pl.BlockSpec((tf, d), lambda i, j: (j, 0)),
        ],
        out_specs=pl.BlockSpec((tm, d), lambda i, j: (i, 0)),
        out_shape=jax.ShapeDtypeStruct((m, d), F32),
        scratch_shapes=[pltpu.VMEM((tm, d), BF16), pltpu.VMEM((tm, d), F32)],
        compiler_params=_params(("parallel", "arbitrary")),
        name="half_ffn",
    )(x, g, wg, wu, wd)


def _proj_kernel(x_ref, g_ref, w_ref, k_ref, v_ref, ki_ref, wi_ref, qa_ref, qb_ref, kb_ref, vb_ref,
                 qib_ref, kia_ref, kib_ref, us_ref, ca_ref, cg_ref, pp_ref):
    h = _rms_bf16(x_ref[...], g_ref[...])
    z = _dot(h, w_ref[...])
    q = z[:, 0:256] * (HEAD_DIM ** -0.5)
    lane = lax.broadcasted_iota(I32, q.shape, 1)
    even = ((lane // HEAD_DIM) % 2) == 0
    qa_ref[...] = jnp.where(even, q, 0.0).astype(BF16)
    qb_ref[...] = jnp.where(even, 0.0, q).astype(BF16)
    k = z[:, 256:512]
    v = z[:, 512:768]
    k_ref[...] = k
    v_ref[...] = v
    kb_ref[...] = k.astype(BF16)
    vb_ref[...] = v.astype(BF16)
    qib_ref[...] = z[:, 768:1024].astype(BF16)
    kia = z[:, 1024:1152]
    ki_ref[...] = kia
    kia_ref[...] = kia.astype(BF16)
    kib_ref[...] = z[:, 1152:1280].astype(BF16)
    wi_ref[...] = z[:, 1280:1408]
    us_ref[...] = z[:, 1408:1664]
    ca_ref[...] = z[:, 1664:1920]
    cg_ref[...] = z[:, 1920:2176]
    pp_ref[...] = z[:, 2176:2432]


def _proj(x, g, w, tm):
    m, d = x.shape
    row = lambda i: (i, 0)
    f256 = jax.ShapeDtypeStruct((m, 256), F32)
    f128 = jax.ShapeDtypeStruct((m, 128), F32)
    b256 = jax.ShapeDtypeStruct((m, 256), BF16)
    b128 = jax.ShapeDtypeStruct((m, 128), BF16)
    shapes = [f256, f256, f128, f128, b256, b256, b256, b256, b256, b128, b128, f256, f256, f256, f256]
    return pl.pallas_call(
        _proj_kernel,
        grid=(m // tm,),
        in_specs=[
            pl.BlockSpec((tm, d), row),
            pl.BlockSpec((1, d), lambda i: (0, 0)),
            pl.BlockSpec((d, PROJ_COLS), lambda i: (0, 0)),
        ],
        out_specs=[pl.BlockSpec((tm, s.shape[1]), row) for s in shapes],
        out_shape=shapes,
        compiler_params=_params(("parallel",)),
        name="mixer_in_proj",
    )(x, g, w)


def _sortable_key(s):
    b = lax.bitcast_convert_type(s, I32)
    k = b ^ ((b >> 31) & 0x7FFFFFFF)
    return jnp.where(k == -1, 0, k)


def _lane_total(pc):
    ones = jnp.ones((BLK, BLK), BF16)
    return _dot(pc.astype(F32).astype(BF16), ones)


def _nth_largest_key(ks_ref, nkb, rows, n_sel):
    def bit_body(bi, t):
        c = t | jnp.left_shift(jnp.int32(1), 31 - bi)
        cs = c ^ INT_MIN

        def blk(j, pc):
            return pc + jnp.where(ks_ref[j] >= cs, 1, 0)

        pc = lax.fori_loop(0, nkb, blk, jnp.zeros((rows, BLK), I32))
        return jnp.where(_lane_total(pc) >= n_sel, c, t)

    t = lax.fori_loop(0, 32, bit_body, jnp.zeros((rows, BLK), I32))
    return t ^ INT_MIN


def _write_select_mask(ks_ref, am_ref, nkb, rows, n_sel):
    thr = _nth_largest_key(ks_ref, nkb, rows, n_sel)

    def cnt_blk(j, pc):
        return pc + jnp.where(ks_ref[j] > thr, 1, 0)

    n_gt = _lane_total(lax.fori_loop(0, nkb, cnt_blk, jnp.zeros((rows, BLK), I32)))
    need = n_sel - n_gt
    r = lax.broadcasted_iota(I32, (BLK, 2 * BLK), 0)
    c = lax.broadcasted_iota(I32, (BLK, 2 * BLK), 1)
    tri_ones = jnp.where((r <= c) | (c >= BLK), 1.0, 0.0).astype(BF16)

    def mask_blk(j, seen):
        kk = ks_ref[j]
        eq = kk == thr
        pr = _dot(jnp.where(eq, 1.0, 0.0).astype(BF16), tri_ones)
        rank = seen + pr[:, :BLK]
        tie = jnp.where(eq, jnp.where(rank <= need, 0.0, NEG), NEG)
        am = jnp.where(kk > thr, 0.0, tie)
        am_ref[j] = jnp.where(kk == KEY_NEG_INF, NEG, am)
        return seen + pr[:, BLK:]

    lax.fori_loop(0, nkb, mask_blk, jnp.zeros((rows, BLK), F32))


def _fold(x, op):
    x3 = x.reshape(x.shape[0] // SUB, SUB, x.shape[1])
    return op(x3, axis=0)


def _tree_sum(parts):
    while len(parts) > 1:
        parts = [parts[n] + parts[n + 1] for n in range(0, len(parts) - 1, 2)] + parts[len(parts) & ~1:]
    return parts[0]


def _count16(ref, nkb, ref_val, strict):
    nq = ref.shape[2]
    cs = jnp.broadcast_to(ref_val, (HALF_ROWS, nq)).astype(I16)[None]
    one, zero = jnp.int16(1), jnp.int16(0)

    def blk(j, pc):
        x3 = ref[j].reshape(KB // HALF_ROWS, HALF_ROWS, nq)
        m3 = jnp.where(x3 > cs if strict else x3 >= cs, one, zero)
        return pc + _tree_sum([m3[g] for g in range(KB // HALF_ROWS)])

    pc = lax.fori_loop(0, nkb, blk, jnp.zeros((HALF_ROWS, nq), I16))
    return jnp.sum(pc.astype(I32), axis=0, keepdims=True)


def _nth_largest16(ref, nkb, n_need):
    nq = ref.shape[2]

    def bit_body(bi, t):
        c = t | jnp.left_shift(jnp.int32(1), 15 - bi)
        return jnp.where(_count16(ref, nkb, c - HALF_BIAS, False) >= n_need, c, t)

    return lax.fori_loop(0, 16, bit_body, jnp.zeros((1, nq), I32)) - HALF_BIAS


def _select_mask_t(ks_ref, hi_ref, lo_ref, am_ref, nkb, n_sel):
    nq = ks_ref.shape[2]
    t_hi = _nth_largest16(hi_ref, nkb, n_sel)
    n_above = _count16(hi_ref, nkb, t_hi, True)
    t_hi16 = jnp.broadcast_to(t_hi, (HALF_ROWS, nq)).astype(I16)[None]

    def park(j, carry):
        shape3 = (KB // HALF_ROWS, HALF_ROWS, nq)
        lo3 = jnp.where(hi_ref[j].reshape(shape3) == t_hi16, lo_ref[j].reshape(shape3), jnp.int16(-HALF_BIAS))
        lo_ref[j] = lo3.reshape(KB, nq)
        return carry

    lax.fori_loop(0, nkb, park, 0)
    t_lo = _nth_largest16(lo_ref, nkb, n_sel - n_above)
    thr = jnp.left_shift(t_hi, 16) | (t_lo + HALF_BIAS)
    need = (n_sel - n_above - _count16(lo_ref, nkb, t_lo, True)).astype(F32)
    r = lax.broadcasted_iota(I32, (KB, KB), 0)
    c = lax.broadcasted_iota(I32, (KB, KB), 1)
    tri = jnp.where(c <= r, 1.0, 0.0).astype(BF16)

    def mask_blk(j, seen):
        kk = ks_ref[j]
        eq = kk == thr
        pre = _dot(tri, jnp.where(eq, 1.0, 0.0).astype(BF16))
        tie = jnp.where(eq, jnp.where(seen + pre <= need, 0.0, NEG), NEG)
        am = jnp.where(kk > thr, 0.0, tie)
        am_ref[j] = jnp.where(kk == KEY_NEG_INF, NEG, am)
        return seen + pre[KB - 1:KB, :]

    lax.fori_loop(0, nkb, mask_blk, jnp.zeros((1, nq), F32))


def _pattn_kernel(qa_ref, qb_ref, qib_ref, wi_ref, kb_ref, vb_ref, kia_ref, kib_ref, tbl_ref, o_ref,
                  ks_scr, hi_scr, lo_scr, am_scr, vt_scr, acc_scr, p_scr, m_scr, *, n_sel):
    i = pl.program_id(1)
    n_full = i * KPQ
    nkb = n_full + KPQ
    half = D_GROUP // 2

    @pl.when(i == 0)
    def _():
        def tr(j, carry):
            ksl = pl.ds(pl.multiple_of(j * KB, KB), KB)
            vt_scr[j] = vb_ref[ksl, :].astype(F32).T.astype(BF16)
            return carry
        lax.fori_loop(0, vt_scr.shape[0], tr, 0)

    wit = wi_ref[...].T
    q01 = qib_ref[:, 0:half]
    q23 = qib_ref[:, half:D_GROUP]
    kpos0 = lax.broadcasted_iota(I32, (KB, QB), 0)
    qpos = i * QB + lax.broadcasted_iota(I32, (KB, QB), 1)

    def score_blk(diagonal):
        def body(j, carry):
            ksl = pl.ds(pl.multiple_of(j * KB, KB), KB)
            ka = kia_ref[ksl, :]
            kb2 = kib_ref[ksl, :]
            s = wit[0:1, :] * jnp.maximum(_dot_nt(ka, q01), 0.0)
            s = s + wit[1:2, :] * jnp.maximum(_dot_nt(kb2, q01), 0.0)
            s = s + wit[2:3, :] * jnp.maximum(_dot_nt(ka, q23), 0.0)
            s = s + wit[3:4, :] * jnp.maximum(_dot_nt(kb2, q23), 0.0)
            if diagonal:
                s = jnp.where(j * KB + kpos0 <= qpos, s, -jnp.inf)
            key = _sortable_key(s)
            ks_scr[j] = key
            hi_scr[j] = (key >> 16).astype(I16)
            lo_scr[j] = ((key & 0xFFFF) - HALF_BIAS).astype(I16)
            return carry
        return body

    lax.fori_loop(0, n_full, score_blk(False), 0)
    for d in range(KPQ):
        score_blk(True)(n_full + d, 0)
    _select_mask_t(ks_scr, hi_scr, lo_scr, am_scr, nkb, n_sel)

    qh = (qa_ref[:, 0:half], qb_ref[:, 0:half], qa_ref[:, half:D_GROUP], qb_ref[:, half:D_GROUP])

    def p_blk(near):
        def body(j, carry):
            ms, ls = carry
            ksl = pl.ds(pl.multiple_of(j * KB, KB), KB)
            new_m, new_l = [], []
            for h in range(N_HEADS):
                kp = kb_ref[ksl, 0:half] if h < 2 else kb_ref[ksl, half:D_GROUP]
                lg = _dot_nt(kp, qh[h]) + am_scr[j]
                if near:
                    lg = lg + tbl_ref[h, nkb - 1 - j]
                m_new = jnp.maximum(ms[h], jnp.max(_fold(lg, jnp.max), axis=0, keepdims=True))
                p = jnp.exp(lg - m_new)
                new_l.append(ls[h] * jnp.exp(ms[h] - m_new) + _fold(p, jnp.sum))
                new_m.append(m_new)
                p_scr[h, j] = p.astype(BF16)
                m_scr[h, j] = jnp.broadcast_to(m_new, (SUB, QB))
            return tuple(new_m), tuple(new_l)
        return body

    n_far = jnp.maximum(n_full - 1, 0)
    carry = (tuple(jnp.full((1, QB), -3e38, F32) for _ in range(N_HEADS)),
             tuple(jnp.zeros((SUB, QB), F32) for _ in range(N_HEADS)))
    carry = lax.fori_loop(0, n_far, p_blk(False), carry)
    ms, ls = lax.fori_loop(n_far, nkb, p_blk(True), carry)
    acc_scr[...] = jnp.zeros_like(acc_scr)

    def pv_blk(j, carry):
        for h in range(N_HEADS):
            vt = vt_scr[j, 0:half, :] if h < 2 else vt_scr[j, half:D_GROUP, :]
            acc_scr[h] += _dot(vt, p_scr[h, j]) * jnp.exp(m_scr[h, j, 0:1, :] - ms[h])
        return carry

    lax.fori_loop(0, nkb, pv_blk, 0)
    rows = []
    for h in range(N_HEADS):
        lo = (h % 2) * HEAD_DIM
        rows.append(acc_scr[h, lo:lo + HEAD_DIM, :] / jnp.sum(ls[h], axis=0, keepdims=True))
    o_ref[...] = jnp.concatenate(rows, axis=0).T


def _prompt_attention(pr, tbl, nb, lp, n_sel):
    nqb = lp // QB
    nkt = lp // KB
    m = nb * lp
    qblk = lambda c: pl.BlockSpec((QB, c), lambda b, i: (b * nqb + i, 0))
    kres = lambda c: pl.BlockSpec((lp, c), lambda b, i: (b, 0))
    return pl.pallas_call(
        functools.partial(_pattn_kernel, n_sel=n_sel),
        grid=(nb, nqb),
        in_specs=[qblk(256), qblk(256), qblk(256), qblk(128), kres(256), kres(256), kres(128), kres(128),
                  pl.BlockSpec((N_HEADS, KPQ + 1, KB, QB), lambda b, i: (0, 0, 0, 0))],
        out_specs=qblk(256),
        out_shape=jax.ShapeDtypeStruct((m, 256), F32),
        scratch_shapes=[pltpu.VMEM((nkt, KB, QB), I32), pltpu.VMEM((nkt, KB, QB), I16),
                        pltpu.VMEM((nkt, KB, QB), I16), pltpu.VMEM((nkt, KB, QB), F32),
                        pltpu.VMEM((nkt, D_GROUP, KB), BF16), pltpu.VMEM((N_HEADS, D_GROUP // 2, QB), F32),
                        pltpu.VMEM((N_HEADS, nkt, KB, QB), BF16), pltpu.VMEM((N_HEADS, nkt, SUB, QB), F32)],
        compiler_params=_params(("parallel", "arbitrary")),
        name="prompt_sparse_attention",
    )(pr["qa"], pr["qb"], pr["qib"], pr["wi"], pr["kb"], pr["vb"], pr["kia"], pr["kib"], tbl)


def _ssm_tail(y_lin, u, vec_ref, wglu_ref):
    y = y_lin + vec_ref[0:1, :] * u
    z = _gelu_tanh(y)
    return z * _sigmoid(_dot(z.astype(BF16), wglu_ref[...]))


def _conv_tail(acc, vec_ref):
    y = acc + vec_ref[1:2, :]
    mu = jnp.mean(y, axis=-1, keepdims=True)
    var = jnp.mean(jnp.square(y - mu), axis=-1, keepdims=True)
    y = (y - mu) * lax.rsqrt(var + EPS) * vec_ref[2:3, :] + vec_ref[3:4, :]
    return y * _sigmoid(y)


def _pool_pick(sums, cnts, cur):
    lane = lax.broadcasted_iota(I32, cur.shape, 1)
    grp = lane // (D_GROUP // len(POOL_WINDOWS))
    d = sums[3] / cnts[3]
    for g in (2, 1, 0):
        d = jnp.where(grp == g, sums[g] / cnts[g], d)
    return d - cur


def _pmix_kernel(us_ref, ca_ref, cg_ref, pp_ref, bbr_ref, bbi_ref, cre_ref, cim_ref, coef_ref, vec_ref,
                 wglu_ref, cw_ref, pw_ref,
                 yssm_ref, yconv_ref, ypool_ref, u_ref, hfin_ref,
                 hr_scr, hi_scr, car_scr, cext_scr, pext_scr, *, fin_chunk, fin_grp):
    i = pl.program_id(1)
    rows = BLK

    @pl.when(i == 0)
    def _():
        car_scr[...] = jnp.zeros_like(car_scr)
        cext_scr[0:32, :] = jnp.zeros((32, D_GROUP), F32)
        pext_scr[0:16, :] = jnp.zeros((16, D_GROUP), F32)

    us = us_ref[...]
    ub = us.astype(BF16)
    hr_scr[...] = _dot(ub, bbr_ref[...])
    hi_scr[...] = _dot(ub, bbi_ref[...])
    k1r, k1i, k2r, k2i, k4r, k4i, pwr, pwi = (coef_ref[n] for n in range(8))

    def grp_body(r, carry):
        cr, ci = carry
        sl = pl.ds(pl.multiple_of(r * SUB, SUB), SUB)
        xr = hr_scr[sl, :]
        xi = hi_scr[sl, :]
        for sh, (ar, ai) in ((1, (k1r, k1i)), (2, (k2r, k2i)), (4, (k4r, k4i))):
            sr = pltpu.roll(xr, sh, 0)
            si = pltpu.roll(xi, sh, 0)
            xr, xi = xr + ar * sr - ai * si, xi + ar * si + ai * sr
        hr = xr + pwr * cr - pwi * ci
        hi = xi + pwr * ci + pwi * cr
        hr_scr[sl, :] = hr
        hi_scr[sl, :] = hi
        return (jnp.broadcast_to(hr[SUB - 1:SUB, :], hr.shape), jnp.broadcast_to(hi[SUB - 1:SUB, :], hi.shape))

    cr, ci = lax.fori_loop(0, rows // SUB, grp_body, (car_scr[0], car_scr[1]))
    car_scr[0] = cr
    car_scr[1] = ci

    @pl.when(i == fin_chunk)
    def _():
        hfin_ref[0] = hr_scr[fin_grp * SUB:(fin_grp + 1) * SUB, :]
        hfin_ref[1] = hi_scr[fin_grp * SUB:(fin_grp + 1) * SUB, :]

    y_lin = _dot(hr_scr[...].astype(BF16), cre_ref[...]) - _dot(hi_scr[...].astype(BF16), cim_ref[...])
    yssm_ref[...] = _ssm_tail(y_lin, us, vec_ref, wglu_ref)

    u = ca_ref[...] * _sigmoid(cg_ref[...])
    u_ref[...] = u
    cext_scr[32:32 + rows, :] = u
    acc = jnp.zeros((rows, D_GROUP), F32)
    for k in range(CONV_WIDTH):
        acc = acc + cw_ref[k:k + 1, :] * cext_scr[2 + k:2 + k + rows, :]
    yconv_ref[...] = _conv_tail(acc, vec_ref)
    cext_scr[0:32, :] = cext_scr[rows:rows + 32, :]

    p = pp_ref[...]
    pext_scr[16:16 + rows, :] = p
    e = pext_scr[...]
    s2 = e + pltpu.roll(e, 1, 0)
    s4 = s2 + pltpu.roll(s2, 2, 0)
    s8 = s4 + pltpu.roll(s4, 4, 0)
    s16 = s8 + pltpu.roll(s8, 8, 0)
    pos1 = (i * rows + 1 + lax.broadcasted_iota(I32, (rows, D_GROUP), 0)).astype(F32)
    sums = [s[16:16 + rows, :] for s in (s2, s4, s8, s16)]
    cnts = [jnp.minimum(pos1, float(w)) for w in POOL_WINDOWS]
    d = _pool_pick(sums, cnts, p)
    ypool_ref[...] = _dot(d.astype(BF16), pw_ref[...]) * vec_ref[4:5, :]
    pext_scr[0:16, :] = pext_scr[rows:rows + 16, :]


def _prompt_mixers(pr, sp, nb, lp, t_real):
    nqb = lp // BLK
    m = nb * lp
    blk = pl.BlockSpec((BLK, D_GROUP), lambda b, i: (b * nqb + i, 0))
    full = lambda a: pl.BlockSpec(a.shape, lambda b, i: (0,) * a.ndim)
    last = t_real - 1
    weights = [sp["bbr"], sp["bbi"], sp["cre"], sp["cim"], sp["coef"], sp["vec"], sp["wglu"], sp["cw"], sp["pw"]]
    f256 = jax.ShapeDtypeStruct((m, D_GROUP), F32)
    return pl.pallas_call(
        functools.partial(_pmix_kernel, fin_chunk=last // BLK, fin_grp=(last % BLK) // SUB),
        grid=(nb, nqb),
        in_specs=[blk, blk, blk, blk] + [full(w) for w in weights],
        out_specs=[blk, blk, blk, blk, pl.BlockSpec((None, 2, SUB, SSM_LANES), lambda b, i: (b, 0, 0, 0))],
        out_shape=[f256, f256, f256, f256, jax.ShapeDtypeStruct((nb, 2, SUB, SSM_LANES), F32)],
        scratch_shapes=[pltpu.VMEM((BLK, SSM_LANES), F32), pltpu.VMEM((BLK, SSM_LANES), F32),
                        pltpu.VMEM((2, SUB, SSM_LANES), F32), pltpu.VMEM((32 + BLK, D_GROUP), F32),
                        pltpu.VMEM((16 + BLK, D_GROUP), F32)],
        compiler_params=_params(("parallel", "arbitrary")),
        name="prompt_ssm_conv_pool",
    )(pr["us"], pr["ca"], pr["cg"], pr["pp"], *weights)


def _smix_kernel(us_ref, ca_ref, cg_ref, pp_ref, h0r_ref, h0i_ref, cst_ref, pst_ref,
                 bbr_ref, bbi_ref, cre_ref, cim_ref, ab_ref, vec_ref, wglu_ref, cw_ref, pw_ref,
                 yssm_ref, yconv_ref, ypool_ref, u_ref, hr_ref, hi_ref, *, t_new):
    c = D_GROUP
    sl = lambda t: slice(t * c, (t + 1) * c)
    ar = ab_ref[0:1, :]
    ai = ab_ref[1:2, :]
    hr = h0r_ref[...]
    hi = h0i_ref[...]
    for t in range(t_new):
        us = us_ref[:, sl(t)]
        ub = us.astype(BF16)
        hr, hi = ar * hr - ai * hi + _dot(ub, bbr_ref[...]), ar * hi + ai * hr + _dot(ub, bbi_ref[...])
        y_lin = _dot(hr.astype(BF16), cre_ref[...]) - _dot(hi.astype(BF16), cim_ref[...])
        yssm_ref[:, sl(t)] = _ssm_tail(y_lin, us, vec_ref, wglu_ref)
    hr_ref[...] = hr
    hi_ref[...] = hi

    n_cst = CONV_WIDTH - 1
    ext = [cst_ref[:, sl(j)] for j in range(n_cst)]
    for t in range(t_new):
        u = ca_ref[:, sl(t)] * _sigmoid(cg_ref[:, sl(t)])
        u_ref[:, sl(t)] = u
        ext.append(u)
    for t in range(t_new):
        acc = cw_ref[0:1, :] * ext[t]
        for k in range(1, CONV_WIDTH):
            acc = acc + cw_ref[k:k + 1, :] * ext[t + k]
        yconv_ref[:, sl(t)] = _conv_tail(acc, vec_ref)

    pe = [pst_ref[:, sl(j)] for j in range(POOL_BUF)] + [pp_ref[:, sl(t)] for t in range(t_new)]
    n = len(pe)
    s2 = [None] + [pe[j] + pe[j - 1] for j in range(1, n)]
    s4 = [None] * 3 + [s2[j] + s2[j - 2] for j in range(3, n)]
    s8 = [None] * 7 + [s4[j] + s4[j - 4] for j in range(7, n)]
    s16 = [None] * 15 + [s8[j] + s8[j - 8] for j in range(15, n)]
    for t in range(t_new):
        j = POOL_BUF + t
        cnts = [float(w) for w in POOL_WINDOWS]
        d = _pool_pick([s2[j], s4[j], s8[j], s16[j]], cnts, pe[j])
        ypool_ref[:, sl(t)] = _dot(d.astype(BF16), pw_ref[...]) * vec_ref[4:5, :]


def _sample_mixers(sm, sp, h0r, h0i, cst, pst, nseq, t_new):
    wide = lambda a: a.reshape(nseq, t_new * D_GROUP)
    args = [wide(sm["us"]), wide(sm["ca"]), wide(sm["cg"]), wide(sm["pp"]), h0r, h0i, cst, pst,
            sp["bbr"], sp["bbi"], sp["cre"], sp["cim"], sp["ab"], sp["vec"], sp["wglu"], sp["cw"], sp["pw"]]
    full = lambda a: pl.BlockSpec(a.shape, lambda i: (0,) * a.ndim)
    fw = jax.ShapeDtypeStruct((nseq, t_new * D_GROUP), F32)
    fs = jax.ShapeDtypeStruct((nseq, SSM_LANES), F32)
    shapes = [fw, fw, fw, fw, fs, fs]
    return pl.pallas_call(
        functools.partial(_smix_kernel, t_new=t_new),
        grid=(1,),
        in_specs=[full(a) for a in args],
        out_specs=[full(s) for s in shapes],
        out_shape=shapes,
        compiler_params=_params(("arbitrary",)),
        name="sample_ssm_conv_pool",
    )(*args)


def _sscore_kernel(pt_ref, qib_ref, wi_ref, kin_ref, *rest, n_pages, t_new):
    page_refs = rest[:n_pages]
    s_ref = rest[n_pages]
    qif = qib_ref[...].astype(F32)
    qst = jnp.concatenate([qif[:, h * IDX_DIM:(h + 1) * IDX_DIM] for h in range(IDX_HEADS)], axis=0).astype(BF16)
    wcol = jnp.concatenate([wi_ref[:, h:h + 1] for h in range(IDX_HEADS)], axis=0)

    def score(qk):
        d = wcol * jnp.maximum(qk, 0.0)
        s = d[0:t_new]
        for h in range(1, IDX_HEADS):
            s = s + d[h * t_new:(h + 1) * t_new]
        return s

    for p in range(n_pages):
        s_ref[:, p * PAGE:(p + 1) * PAGE] = score(_dot(qst, page_refs[p][...].astype(BF16)))
    knew = jnp.concatenate([kin_ref[...].astype(F32)[:, 0:IDX_DIM], jnp.zeros((PAGE - t_new, IDX_DIM), F32)], axis=0)
    s = score(_dot_nt(qst, knew.astype(BF16)))
    r = lax.broadcasted_iota(I32, s.shape, 0)
    c = lax.broadcasted_iota(I32, s.shape, 1)
    s_ref[:, n_pages * PAGE:(n_pages + 1) * PAGE] = jnp.where(c <= r, s, -jnp.inf)


def _sample_scores(sm, cache_kidx, pt_flat, layer, nseq, t_new, n_pages):
    ncol = (n_pages + 1) * PAGE
    row = lambda c: pl.BlockSpec((t_new, c), lambda b, pt: (b, 0))
    page = lambda p: pl.BlockSpec((None, None, IDX_DIM, PAGE),
                                  lambda b, pt, p=p: (layer, pt[b * n_pages + p], 0, 0))
    grid_spec = pltpu.PrefetchScalarGridSpec(
        num_scalar_prefetch=1, grid=(nseq,),
        in_specs=[row(256), row(128), row(128)] + [page(p) for p in range(n_pages)],
        out_specs=pl.BlockSpec((None, t_new, ncol), lambda b, pt: (b, 0, 0)))
    return pl.pallas_call(
        functools.partial(_sscore_kernel, n_pages=n_pages, t_new=t_new),
        grid_spec=grid_spec,
        out_shape=jax.ShapeDtypeStruct((nseq, t_new, ncol), F32),
        compiler_params=_params(("parallel",)),
        name="sample_indexer_scores",
    )(pt_flat, sm["qib"], sm["wi"], sm["kia"], *([cache_kidx] * n_pages))


def _sselect_kernel(s_ref, am_ref, ks_scr, am_scr, *, ncb, n_sel):
    for j in range(ncb):
        ks_scr[j] = _sortable_key(s_ref[:, j * BLK:(j + 1) * BLK])
    _write_select_mask(ks_scr, am_scr, ncb, BLK, n_sel)
    for j in range(ncb):
        am_ref[:, j * BLK:(j + 1) * BLK] = am_scr[j]


def _sample_select(scores, n_sel):
    m, ncol = scores.shape
    ncb = ncol // BLK
    spec = pl.BlockSpec((BLK, ncol), lambda i: (i, 0))
    return pl.pallas_call(
        functools.partial(_sselect_kernel, ncb=ncb, n_sel=n_sel),
        grid=(m // BLK,),
        in_specs=[spec],
        out_specs=spec,
        out_shape=jax.ShapeDtypeStruct((m, ncol), F32),
        scratch_shapes=[pltpu.VMEM((ncb, BLK, BLK), I32), pltpu.VMEM((ncb, BLK, BLK), F32)],
        compiler_params=_params(("parallel",)),
        name="sample_topn_mask",
    )(scores)


def _sattn_kernel(pt_ref, q_ref, am_ref, kn_ref, vn_ref, tbl_ref, *rest, n_pages, t_new):
    k_refs = rest[:n_pages]
    v_refs = rest[n_pages:2 * n_pages]
    o_ref = rest[2 * n_pages]
    q = q_ref[...].astype(F32)
    lane = lax.broadcasted_iota(I32, q.shape, 1)
    qbd = jnp.concatenate([jnp.where(lane // HEAD_DIM == h, q, 0.0) for h in range(N_HEADS)], axis=0).astype(BF16)
    zpad = jnp.zeros((PAGE - t_new, D_GROUP), F32)
    pad_page = lambda r: jnp.concatenate([r[...].astype(F32), zpad], axis=0).astype(BF16)
    lgs = []
    for p in range(n_pages + 1):
        am = am_ref[:, p * PAGE:(p + 1) * PAGE]
        am4 = jnp.concatenate([am] * N_HEADS, axis=0)
        tsel = 0 if p < n_pages - 1 else (1 if p == n_pages - 1 else 2)
        qk = _dot(qbd, k_refs[p][...].astype(BF16)) if p < n_pages else _dot_nt(qbd, pad_page(kn_ref))
        lgs.append(qk + tbl_ref[tsel] + am4)
    mx = lgs[0]
    for lg in lgs[1:]:
        mx = jnp.maximum(mx, lg)
    mx = jnp.max(mx, axis=-1, keepdims=True)
    ps = [jnp.exp(lg - mx) for lg in lgs]
    tot = ps[0]
    for pp_ in ps[1:]:
        tot = tot + pp_
    den = jnp.sum(tot, axis=-1, keepdims=True)
    acc = _dot(ps[n_pages].astype(BF16), pad_page(vn_ref))
    for p in range(n_pages):
        acc = acc + _dot_nt(ps[p].astype(BF16), v_refs[p][...].astype(BF16))
    acc = acc / den
    out = acc[0:t_new]
    for h in range(1, N_HEADS):
        out = jnp.where(lane // HEAD_DIM == h, acc[h * t_new:(h + 1) * t_new], out)
    o_ref[...] = out


def _sample_attention(sm, qs, am, tbl, cache_k, cache_v, pt_flat, layer, nseq, t_new, n_pages):
    ncol = (n_pages + 1) * PAGE
    row = lambda c: pl.BlockSpec((t_new, c), lambda b, pt: (b, 0))
    page = lambda p: pl.BlockSpec((None, None, D_GROUP, PAGE),
                                  lambda b, pt, p=p: (layer, pt[b * n_pages + p], 0, 0))
    grid_spec = pltpu.PrefetchScalarGridSpec(
        num_scalar_prefetch=1, grid=(nseq,),
        in_specs=[row(256), row(ncol), row(256), row(256),
                  pl.BlockSpec(tbl.shape, lambda b, pt: (0, 0, 0))]
        + [page(p) for p in range(n_pages)] + [page(p) for p in range(n_pages)],
        out_specs=row(256))
    return pl.pallas_call(
        functools.partial(_sattn_kernel, n_pages=n_pages, t_new=t_new),
        grid_spec=grid_spec,
        out_shape=jax.ShapeDtypeStruct((nseq * t_new, D_GROUP), F32),
        compiler_params=_params(("parallel",)),
        name="sample_sparse_attention",
    )(pt_flat, qs, am, sm["kb"], sm["vb"], tbl, *([cache_k] * n_pages), *([cache_v] * n_pages))


def _outproj_kernel(x_ref, a_ref, s_ref, c_ref, p_ref, w_ref, o_ref):
    acc = x_ref[...]
    for n, r in enumerate((a_ref, s_ref, c_ref, p_ref)):
        acc = acc + _dot(r[...].astype(BF16), w_ref[n * D_GROUP:(n + 1) * D_GROUP, :])
    o_ref[...] = acc


def _outproj(x, ys, w, tm):
    m, d = x.shape
    row = lambda c: pl.BlockSpec((tm, c), lambda i: (i, 0))
    return pl.pallas_call(
        _outproj_kernel,
        grid=(m // tm,),
        in_specs=[row(d)] + [row(D_GROUP)] * 4 + [pl.BlockSpec(w.shape, lambda i: (0, 0))],
        out_specs=row(d),
        out_shape=jax.ShapeDtypeStruct((m, d), F32),
        compiler_params=_params(("parallel",)),
        name="mixer_out_proj",
    )(x, *ys, w)


def _rmsnorm_kernel(x_ref, g_ref, o_ref):
    x = x_ref[...]
    ms = jnp.mean(x * x, axis=-1, keepdims=True)
    o_ref[...] = x * lax.rsqrt(ms + EPS) * g_ref[...]


def _rmsnorm(x, g, tm):
    m, d = x.shape
    return pl.pallas_call(
        _rmsnorm_kernel,
        grid=(m // tm,),
        in_specs=[pl.BlockSpec((tm, d), lambda i: (i, 0)), pl.BlockSpec((1, d), lambda i: (0, 0))],
        out_specs=pl.BlockSpec((tm, d), lambda i: (i, 0)),
        out_shape=jax.ShapeDtypeStruct((m, d), F32),
        compiler_params=_params(("parallel",)),
        name="final_rmsnorm",
    )(x, g)


def _bucket_table(dist):
    n = np.maximum(dist, 0)
    max_exact = N_BUCKETS // 2
    nf = np.maximum(n, 1).astype(np.float32)
    large = max_exact + (np.log(nf / np.float32(max_exact)) / np.float32(math.log(MAX_DISTANCE / max_exact))
                         * np.float32(N_BUCKETS - max_exact)).astype(np.int32)
    large = np.minimum(large, N_BUCKETS - 1)
    return np.where(n < max_exact, n, large).astype(np.int32)


def _bias_tables(rel_bias, t_new):
    kk = np.arange(KB)[:, None]
    qq = np.arange(QB)[None, :]
    near = [_bucket_table((t - (KPQ - 1)) * KB + qq - kk) for t in range(KPQ + 1)]
    pidx = np.stack(near + [_bucket_table(np.full((KB, QB), 2 * KB + 1))])
    ptbl = jnp.transpose(rel_bias[pidx], (3, 0, 1, 2))
    ptbl = ptbl[:, :-1] - ptbl[:, -1:]
    c = np.arange(BLK)[None, :]
    t = np.arange(t_new)[:, None]
    sidx = np.stack([_bucket_table(np.full((t_new, BLK), 2 * BLK)), _bucket_table(BLK + t - c),
                     _bucket_table(t - c)])
    stbl = jnp.transpose(rel_bias[sidx], (0, 3, 1, 2)).reshape(3, N_HEADS * t_new, BLK)
    return ptbl.astype(F32), stbl.astype(F32)


def _cmul(a, b):
    return a[0] * b[0] - a[1] * b[1], a[0] * b[1] + a[1] * b[0]


def _ssm_params(a_re, a_im, log_dt, b_re, b_im, c_re, c_im):
    nl = a_re.shape[0]
    dt = jnp.exp(log_dt)[..., None]
    decay = jnp.exp(dt * a_re)
    ab_re = decay * jnp.cos(dt * a_im)
    ab_im = decay * jnp.sin(dt * a_im)
    den = a_re * a_re + a_im * a_im
    f_re = ((ab_re - 1.0) * a_re + ab_im * a_im) / den
    f_im = (ab_im * a_re - (ab_re - 1.0) * a_im) / den
    bb_re = f_re[..., None] * b_re - f_im[..., None] * b_im
    bb_im = f_re[..., None] * b_im + f_im[..., None] * b_re
    eye = jnp.eye(SSM_GROUPS, dtype=F32)
    pack_b = lambda b: jnp.einsum("lgpc,gh->lgchp", b, eye).reshape(nl, D_GROUP, SSM_LANES).astype(BF16)
    pack_c = lambda c: jnp.einsum("lgcp,gh->lgphc", c, eye).reshape(nl, SSM_LANES, D_GROUP).astype(BF16)
    a1 = (ab_re.reshape(nl, SSM_LANES), ab_im.reshape(nl, SSM_LANES))
    a2 = _cmul(a1, a1)
    a3 = _cmul(a2, a1)
    a4 = _cmul(a2, a2)
    a5 = _cmul(a4, a1)
    a6 = _cmul(a4, a2)
    a7 = _cmul(a4, a3)
    a8 = _cmul(a4, a4)
    t = jnp.arange(SUB)[None, :, None]
    gate = lambda a, s: jnp.where(t >= s, a[:, None, :], 0.0)
    pw_re = jnp.stack([a[0] for a in (a1, a2, a3, a4, a5, a6, a7, a8)], axis=1)
    pw_im = jnp.stack([a[1] for a in (a1, a2, a3, a4, a5, a6, a7, a8)], axis=1)
    coef = jnp.stack([gate(a1[0], 1), gate(a1[1], 1), gate(a2[0], 2), gate(a2[1], 2),
                      gate(a4[0], 4), gate(a4[1], 4), pw_re, pw_im], axis=1)
    ab = jnp.stack([a1[0], a1[1]], axis=1)
    return pack_b(bb_re), pack_b(bb_im), pack_c(c_re), pack_c(c_im), coef, ab


def _pad_rows(a, n):
    return jnp.concatenate([a, jnp.zeros((n - a.shape[0],) + a.shape[1:], a.dtype)], axis=0)


def kernel(x_prompt, x_sample, cache_k, cache_v, cache_kidx, page_table, state_ssm_re, state_ssm_im, state_conv, state_pool, meta_tokens, rel_bias, norm_ffn1, ffn1_w_gate, ffn1_w_up, ffn1_w_down, norm_mix, w_in, w_out, ssm_a_re, ssm_a_im, ssm_log_dt, ssm_b_re, ssm_b_im, ssm_c_re, ssm_c_im, ssm_d, ssm_w_glu, conv_w, conv_b, conv_ln_g, conv_ln_b, pool_w, pool_scale, norm_ffn2, ffn2_w_gate, ffn2_w_up, ffn2_w_down, norm_final):
    nb, seq, d = x_prompt.shape
    nseq, t_new, _ = x_sample.shape
    depth = w_in.shape[0]
    n_pages = page_table.shape[1]
    n_pool = cache_k.shape[1]
    t_real = seq + N_META
    lp = -(-t_real // QB) * QB
    past = n_pages * PAGE
    n_sel_p = min(TOPK_MAX, t_real // 4)
    n_sel_s = min(TOPK_MAX, (past + t_new) // 4)
    mp, ms = nb * lp, nseq * t_new
    tm_p, tm_s = 1024, 512
    assert mp % tm_p == 0 and ms % tm_s == 0 and t_new == SUB

    meta = jnp.broadcast_to(meta_tokens[None].astype(F32), (nb, N_META, d))
    xp = jnp.concatenate([meta, x_prompt, jnp.zeros((nb, lp - t_real, d), F32)], axis=1).reshape(mp, d)
    xs = x_sample.reshape(ms, d)

    zc = lambda n: jnp.zeros((depth, d, n), F32)
    q_, k_, v_, qi_, ki_, wi_, rest = (w_in[..., 0:256], w_in[..., 256:512], w_in[..., 512:768], w_in[..., 768:1024],
                                       w_in[..., 1024:1088], w_in[..., 1088:1092], w_in[..., 1092:])
    w_in_p = jnp.concatenate([q_, k_, v_, qi_, ki_, zc(64), zc(64), ki_, wi_, zc(124), rest], axis=-1).astype(BF16)
    w_out_b = w_out.astype(BF16)
    ffn_w = [[w.astype(BF16) for w in ws] for ws in ((ffn1_w_gate, ffn1_w_up, ffn1_w_down),
                                                     (ffn2_w_gate, ffn2_w_up, ffn2_w_down))]
    bbr, bbi, cre, cim, coef, ab = _ssm_params(ssm_a_re, ssm_a_im, ssm_log_dt, ssm_b_re, ssm_b_im, ssm_c_re, ssm_c_im)
    zrow = jnp.zeros((depth, D_GROUP), F32)
    vec = jnp.stack([ssm_d, conv_b, conv_ln_g, conv_ln_b, pool_scale, zrow, zrow, zrow], axis=1)
    cw = jnp.concatenate([conv_w, jnp.zeros((depth, 1, D_GROUP), F32)], axis=1)
    ng = len(POOL_WINDOWS)
    pw = jnp.einsum("lgcd,gh->lgchd", pool_w, jnp.eye(ng, dtype=F32)).reshape(depth, D_GROUP, D_GROUP).astype(BF16)
    wglu = ssm_w_glu.astype(BF16)
    ptbl, stbl = _bias_tables(rel_bias, t_new)

    ck = jnp.transpose(cache_k, (0, 1, 3, 4, 2)).reshape(depth, n_pool, D_GROUP, PAGE)
    cv = jnp.transpose(cache_v, (0, 1, 3, 4, 2)).reshape(depth, n_pool, D_GROUP, PAGE)
    cki = jnp.transpose(cache_kidx, (0, 1, 3, 2))
    pt_flat = page_table.reshape(-1).astype(I32)
    h0r = state_ssm_re.reshape(depth, nseq, SSM_LANES)
    h0i = state_ssm_im.reshape(depth, nseq, SSM_LANES)
    cst = state_conv.reshape(depth, nseq, (CONV_WIDTH - 1) * D_GROUP)
    pst = state_pool.reshape(depth, nseq, POOL_BUF * D_GROUP)
    names = ("k", "v", "ki", "wi", "qa", "qb", "kb", "vb", "qib", "kia", "kib", "us", "ca", "cg", "pp")

    pst_out = [[] for _ in range(7)]
    sst_out = [[] for _ in range(7)]
    for l in range(depth):
        g1 = norm_ffn1[l][None]
        xp = _ffn(xp, g1, ffn_w[0][0][l], ffn_w[0][1][l], ffn_w[0][2][l], tm_p)
        xs = _ffn(xs, g1, ffn_w[0][0][l], ffn_w[0][1][l], ffn_w[0][2][l], tm_s)

        gm = norm_mix[l][None]
        pr = dict(zip(names, _proj(xp, gm, w_in_p[l], tm_p // 2)))
        sm = dict(zip(names, _proj(xs, gm, w_in_p[l], tm_s)))
        sp = dict(bbr=bbr[l], bbi=bbi[l], cre=cre[l], cim=cim[l], coef=coef[l], ab=ab[l], vec=vec[l],
                  wglu=wglu[l], cw=cw[l], pw=pw[l])

        ya_p = _prompt_attention(pr, ptbl, nb, lp, n_sel_p)
        ys_p, yc_p, yp_p, u_p, hfin = _prompt_mixers(pr, sp, nb, lp, t_real)
        xp = _outproj(xp, (ya_p, ys_p, yc_p, yp_p), w_out_b[l], tm_p)

        scores = _sample_scores(sm, cki, pt_flat, l, nseq, t_new, n_pages)
        am = _sample_select(scores.reshape(ms, -1), n_sel_s)
        qs = sm["qa"] + sm["qb"]
        ya_s = _sample_attention(sm, qs, am, stbl, ck, cv, pt_flat, l, nseq, t_new, n_pages)
        ys_s, yc_s, yp_s, u_s, hr_s, hi_s = _sample_mixers(sm, sp, h0r[l], h0i[l], cst[l], pst[l], nseq, t_new)
        wide = lambda a: a.reshape(ms, D_GROUP)
        xs = _outproj(xs, (ya_s, wide(ys_s), wide(yc_s), wide(yp_s)), w_out_b[l], tm_s)

        g2 = norm_ffn2[l][None]
        xp = _ffn(xp, g2, ffn_w[1][0][l], ffn_w[1][1][l], ffn_w[1][2][l], tm_p)
        xs = _ffn(xs, g2, ffn_w[1][0][l], ffn_w[1][1][l], ffn_w[1][2][l], tm_s)

        seq3 = lambda a: a.reshape(nb, lp, -1)[:, :t_real]
        fin_row = (t_real - 1) % SUB
        pst_out[0].append(seq3(pr["k"]).reshape(nb, t_real, N_HEADS, HEAD_DIM))
        pst_out[1].append(seq3(pr["v"]).reshape(nb, t_real, N_HEADS, HEAD_DIM))
        pst_out[2].append(seq3(pr["ki"])[..., :IDX_DIM])
        pst_out[3].append(hfin[:, 0, fin_row].reshape(nb, SSM_GROUPS, SSM_STATE))
        pst_out[4].append(hfin[:, 1, fin_row].reshape(nb, SSM_GROUPS, SSM_STATE))
        pst_out[5].append(seq3(u_p)[:, t_real - (CONV_WIDTH - 1):])
        pst_out[6].append(seq3(pr["pp"])[:, t_real - POOL_BUF:])
        sst_out[0].append(sm["k"].reshape(nseq, t_new, N_HEADS, HEAD_DIM))
        sst_out[1].append(sm["v"].reshape(nseq, t_new, N_HEADS, HEAD_DIM))
        sst_out[2].append(sm["ki"][:, :IDX_DIM].reshape(nseq, t_new, IDX_DIM))
        sst_out[3].append(hr_s.reshape(nseq, SSM_GROUPS, SSM_STATE))
        sst_out[4].append(hi_s.reshape(nseq, SSM_GROUPS, SSM_STATE))
        sst_out[5].append(jnp.concatenate([state_conv[l][:, t_new:], u_s.reshape(nseq, t_new, D_GROUP)], axis=1))
        sst_out[6].append(jnp.concatenate([state_pool[l][:, t_new:], sm["pp"].reshape(nseq, t_new, D_GROUP)], axis=1))

    gf = norm_final[None]
    y_prompt = _rmsnorm(xp, gf, tm_p).reshape(nb, lp, d)[:, N_META:t_real]
    y_sample = _rmsnorm(xs, gf, tm_s).reshape(nseq, t_new, d)
    return (y_prompt, y_sample, *[jnp.stack(a) for a in pst_out], *[jnp.stack(a) for a in sst_out])
```

```python
import functools
import math

import numpy as np
import jax
import jax.numpy as jnp
from jax import lax
from jax.experimental import pallas as pl
from jax.experimental.pallas import tpu as pltpu

F32 = jnp.float32
BF16 = jnp.bfloat16
I32 = jnp.int32
I16 = jnp.int16

N_META = 16
N_HEADS = 4
HEAD_DIM = 64
IDX_HEADS = 4
IDX_DIM = 64
D_GROUP = 256
TOPK_MAX = 256
N_BUCKETS = 32
MAX_DISTANCE = 128
PAGE = 128
SSM_GROUPS = 16
SSM_GROUP_CH = 16
SSM_STATE = 64
SSM_LANES = SSM_GROUPS * SSM_STATE
CONV_WIDTH = 31
POOL_WINDOWS = (2, 4, 8, 16)
POOL_BUF = 15
EPS = 1e-6

BLK = 128
SUB = 8
QB = 256
KB = 256
KPQ = QB // KB
HALF_ROWS = 16
HALF_BIAS = 32768
NEG = -1e30
KEY_NEG_INF = -2139095041
VMEM_LIMIT = 56 * 1024 * 1024

PROJ_COLS = 2432


def _dot(a, b):
    return jnp.dot(a, b, preferred_element_type=F32)


def _dot_nt(a, b):
    return lax.dot_general(a, b, (((1,), (1,)), ((), ())), preferred_element_type=F32)


def _rms_bf16(x, g):
    ms = jnp.mean(x * x, axis=-1, keepdims=True)
    return (x * lax.rsqrt(ms + EPS) * g).astype(BF16)


def _sigmoid(x):
    return 1.0 / (1.0 + jnp.exp(-x))


def _gelu_tanh(x):
    c = math.sqrt(2.0 / math.pi)
    return 0.5 * x * (1.0 + jnp.tanh(c * (x + 0.044715 * (x * x * x))))


def _params(sem):
    return pltpu.CompilerParams(dimension_semantics=sem, vmem_limit_bytes=VMEM_LIMIT)


def _ffn_kernel(x_ref, g_ref, wg_ref, wu_ref, wd_ref, o_ref, h_scr, acc_scr):
    j = pl.program_id(1)

    @pl.when(j == 0)
    def _():
        h_scr[...] = _rms_bf16(x_ref[...], g_ref[...])
        acc_scr[...] = jnp.zeros_like(acc_scr)

    h = h_scr[...]
    g = _dot(h, wg_ref[...])
    u = _dot(h, wu_ref[...])
    a = (g * _sigmoid(g) * u).astype(BF16)
    acc_scr[...] += _dot(a, wd_ref[...])

    @pl.when(j == pl.num_programs(1) - 1)
    def _():
        o_ref[...] = x_ref[...] + 0.5 * acc_scr[...]


def _ffn(x, g, wg, wu, wd, tm, tf=512):
    m, d = x.shape
    dff = wg.shape[1]
    return pl.pallas_call(
        _ffn_kernel,
        grid=(m // tm, dff // tf),
        in_specs=[
            pl.BlockSpec((tm, d), lambda i, j: (i, 0)),
            pl.BlockSpec((1, d), lambda i, j: (0, 0)),
            pl.BlockSpec((d, tf), lambda i, j: (0, j)),
            pl.BlockSpec((d, tf), lambda i, j: (0, j)),
            pl.BlockSpec((tf, d), lambda i, j: (j, 0)),
        ],
        out_specs=pl.BlockSpec((tm, d), lambda i, j: (i, 0)),
        out_shape=jax.ShapeDtypeStruct((m, d), F32),
        scratch_shapes=[pltpu.VMEM((tm, d), BF16), pltpu.VMEM((tm, d), F32)],
        compiler_params=_params(("parallel", "arbitrary")),
        name="half_ffn",
    )(x, g, wg, wu, wd)


def _proj_kernel(x_ref, g_ref, w_ref, k_ref, v_ref, ki_ref, wi_ref, qa_ref, qb_ref, kb_ref, vb_ref,
                 qib_ref, kia_ref, kib_ref, us_ref, ca_ref, cg_ref, pp_ref):
    h = _rms_bf16(x_ref[...], g_ref[...])
    z = _dot(h, w_ref[...])
    q = z[:, 0:256] * (HEAD_DIM ** -0.5)
    lane = lax.broadcasted_iota(I32, q.shape, 1)
    even = ((lane // HEAD_DIM) % 2) == 0
    qa_ref[...] = jnp.where(even, q, 0.0).astype(BF16)
    qb_ref[...] = jnp.where(even, 0.0, q).astype(BF16)
    k = z[:, 256:512]
    v = z[:, 512:768]
    k_ref[...] = k
    v_ref[...] = v
    kb_ref[...] = k.astype(BF16)
    vb_ref[...] = v.astype(BF16)
    qib_ref[...] = z[:, 768:1024].astype(BF16)
    kia = z[:, 1024:1152]
    ki_ref[...] = kia
    kia_ref[...] = kia.astype(BF16)
    kib_ref[...] = z[:, 1152:1280].astype(BF16)
    wi_ref[...] = z[:, 1280:1408]
    us_ref[...] = z[:, 1408:1664]
    ca_ref[...] = z[:, 1664:1920]
    cg_ref[...] = z[:, 1920:2176]
    pp_ref[...] = z[:, 2176:2432]


def _proj(x, g, w, tm):
    m, d = x.shape
    row = lambda i: (i, 0)
    f256 = jax.ShapeDtypeStruct((m, 256), F32)
    f128 = jax.ShapeDtypeStruct((m, 128), F32)
    b256 = jax.ShapeDtypeStruct((m, 256), BF16)
    b128 = jax.ShapeDtypeStruct((m, 128), BF16)
    shapes = [f256, f256, f128, f128, b256, b256, b256, b256, b256, b128, b128, f256, f256, f256, f256]
    return pl.pallas_call(
        _proj_kernel,
        grid=(m // tm,),
        in_specs=[
            pl.BlockSpec((tm, d), row),
            pl.BlockSpec((1, d), lambda i: (0, 0)),
            pl.BlockSpec((d, PROJ_COLS), lambda i: (0, 0)),
        ],
        out_specs=[pl.BlockSpec((tm, s.shape[1]), row) for s in shapes],
        out_shape=shapes,
        compiler_params=_params(("parallel",)),
        name="mixer_in_proj",
    )(x, g, w)


def _sortable_key(s):
    b = lax.bitcast_convert_type(s, I32)
    k = b ^ ((b >> 31) & 0x7FFFFFFF)
    return jnp.where(k == -1, 0, k)


def _fold(x, op):
    x3 = x.reshape(x.shape[0] // SUB, SUB, x.shape[1])
    return op(x3, axis=0)


def _tree_sum(parts):
    while len(parts) > 1:
        parts = [parts[n] + parts[n + 1] for n in range(0, len(parts) - 1, 2)] + parts[len(parts) & ~1:]
    return parts[0]


def _count16(ref, nkb, ref_val, strict):
    nq = ref.shape[2]
    cs = jnp.broadcast_to(ref_val, (HALF_ROWS, nq)).astype(I16)[None]
    one, zero = jnp.int16(1), jnp.int16(0)

    def blk(j, pc):
        x3 = ref[j].reshape(KB // HALF_ROWS, HALF_ROWS, nq)
        m3 = jnp.where(x3 > cs if strict else x3 >= cs, one, zero)
        return pc + _tree_sum([m3[g] for g in range(KB // HALF_ROWS)])

    pc = lax.fori_loop(0, nkb, blk, jnp.zeros((HALF_ROWS, nq), I16))
    return jnp.sum(pc.astype(I32), axis=0, keepdims=True)


def _nth_largest16(ref, nkb, n_need):
    nq = ref.shape[2]

    def bit_body(bi, t):
        c = t | jnp.left_shift(jnp.int32(1), 15 - bi)
        return jnp.where(_count16(ref, nkb, c - HALF_BIAS, False) >= n_need, c, t)

    return lax.fori_loop(0, 16, bit_body, jnp.zeros((1, nq), I32)) - HALF_BIAS


def _select_mask_t(ks_ref, hi_ref, lo_ref, am_ref, nkb, n_sel):
    nq = ks_ref.shape[2]
    t_hi = _nth_largest16(hi_ref, nkb, n_sel)
    n_above = _count16(hi_ref, nkb, t_hi, True)
    t_hi16 = jnp.broadcast_to(t_hi, (HALF_ROWS, nq)).astype(I16)[None]

    def park(j, carry):
        shape3 = (KB // HALF_ROWS, HALF_ROWS, nq)
        lo3 = jnp.where(hi_ref[j].reshape(shape3) == t_hi16, lo_ref[j].reshape(shape3), jnp.int16(-HALF_BIAS))
        lo_ref[j] = lo3.reshape(KB, nq)
        return carry

    lax.fori_loop(0, nkb, park, 0)
    t_lo = _nth_largest16(lo_ref, nkb, n_sel - n_above)
    thr = jnp.left_shift(t_hi, 16) | (t_lo + HALF_BIAS)
    need = (n_sel - n_above - _count16(lo_ref, nkb, t_lo, True)).astype(F32)
    n_ge = n_above + _count16(lo_ref, nkb, t_lo, False)
    all_ties_fit = jnp.max(n_ge) <= n_sel

    @pl.when(all_ties_fit)
    def _():
        def mask_blk(j, carry):
            kk = ks_ref[j]
            am_ref[j] = jnp.where(kk == KEY_NEG_INF, NEG, jnp.where(kk >= thr, 0.0, NEG))
            return carry
        lax.fori_loop(0, nkb, mask_blk, 0)

    @pl.when(jnp.logical_not(all_ties_fit))
    def _():
        r = lax.broadcasted_iota(I32, (KB, KB), 0)
        c = lax.broadcasted_iota(I32, (KB, KB), 1)
        tri = jnp.where(c <= r, 1.0, 0.0).astype(BF16)

        def mask_blk(j, seen):
            kk = ks_ref[j]
            eq = kk == thr
            pre = _dot(tri, jnp.where(eq, 1.0, 0.0).astype(BF16))
            tie = jnp.where(eq, jnp.where(seen + pre <= need, 0.0, NEG), NEG)
            am = jnp.where(kk > thr, 0.0, tie)
            am_ref[j] = jnp.where(kk == KEY_NEG_INF, NEG, am)
            return seen + pre[KB - 1:KB, :]

        lax.fori_loop(0, nkb, mask_blk, jnp.zeros((1, nq), F32))


def _pattn_kernel(qa_ref, qb_ref, qib_ref, wi_ref, kb_ref, vb_ref, kia_ref, kib_ref, tbl_ref, o_ref,
                  ks_scr, hi_scr, lo_scr, am_scr, vt_scr, acc_scr, p_scr, m_scr, *, n_sel):
    i = pl.program_id(1)
    n_full = i * KPQ
    nkb = n_full + KPQ
    half = D_GROUP // 2

    @pl.when(i == 0)
    def _():
        def tr(j, carry):
            ksl = pl.ds(pl.multiple_of(j * KB, KB), KB)
            vt_scr[j] = vb_ref[ksl, :].astype(F32).T.astype(BF16)
            return carry
        lax.fori_loop(0, vt_scr.shape[0], tr, 0)

    wit = wi_ref[...].T
    q01 = qib_ref[:, 0:half]
    q23 = qib_ref[:, half:D_GROUP]
    kpos0 = lax.broadcasted_iota(I32, (KB, QB), 0)
    qpos = i * QB + lax.broadcasted_iota(I32, (KB, QB), 1)

    def score_blk(diagonal):
        def body(j, carry):
            ksl = pl.ds(pl.multiple_of(j * KB, KB), KB)
            ka = kia_ref[ksl, :]
            kb2 = kib_ref[ksl, :]
            s = wit[0:1, :] * jnp.maximum(_dot_nt(ka, q01), 0.0)
            s = s + wit[1:2, :] * jnp.maximum(_dot_nt(kb2, q01), 0.0)
            s = s + wit[2:3, :] * jnp.maximum(_dot_nt(ka, q23), 0.0)
            s = s + wit[3:4, :] * jnp.maximum(_dot_nt(kb2, q23), 0.0)
            if diagonal:
                s = jnp.where(j * KB + kpos0 <= qpos, s, -jnp.inf)
            key = _sortable_key(s)
            ks_scr[j] = key
            hi_scr[j] = (key >> 16).astype(I16)
            lo_scr[j] = ((key & 0xFFFF) - HALF_BIAS).astype(I16)
            return carry
        return body

    lax.fori_loop(0, n_full, score_blk(False), 0)
    for d in range(KPQ):
        score_blk(True)(n_full + d, 0)
    _select_mask_t(ks_scr, hi_scr, lo_scr, am_scr, nkb, n_sel)

    qh = (qa_ref[:, 0:half], qb_ref[:, 0:half], qa_ref[:, half:D_GROUP], qb_ref[:, half:D_GROUP])

    def p_blk(near):
        def body(j, carry):
            ms, ls = carry
            ksl = pl.ds(pl.multiple_of(j * KB, KB), KB)
            new_m, new_l = [], []
            for h in range(N_HEADS):
                kp = kb_ref[ksl, 0:half] if h < 2 else kb_ref[ksl, half:D_GROUP]
                lg = _dot_nt(kp, qh[h]) + am_scr[j]
                if near:
                    lg = lg + tbl_ref[h, nkb - 1 - j]
                m_new = jnp.maximum(ms[h], jnp.max(_fold(lg, jnp.max), axis=0, keepdims=True))
                p = jnp.exp(lg - m_new)
                new_l.append(ls[h] * jnp.exp(ms[h] - m_new) + _fold(p, jnp.sum))
                new_m.append(m_new)
                p_scr[h, j] = p.astype(BF16)
                m_scr[h, j] = jnp.broadcast_to(m_new, (SUB, QB))
            return tuple(new_m), tuple(new_l)
        return body

    n_far = jnp.maximum(n_full - 1, 0)
    carry = (tuple(jnp.full((1, QB), -3e38, F32) for _ in range(N_HEADS)),
             tuple(jnp.zeros((SUB, QB), F32) for _ in range(N_HEADS)))
    carry = lax.fori_loop(0, n_far, p_blk(False), carry)
    ms, ls = lax.fori_loop(n_far, nkb, p_blk(True), carry)
    acc_scr[...] = jnp.zeros_like(acc_scr)

    def pv_blk(j, carry):
        for h in range(N_HEADS):
            vt = vt_scr[j, 0:half, :] if h < 2 else vt_scr[j, half:D_GROUP, :]
            acc_scr[h] += _dot(vt, p_scr[h, j]) * jnp.exp(m_scr[h, j, 0:1, :] - ms[h])
        return carry

    lax.fori_loop(0, nkb, pv_blk, 0)
    rows = []
    for h in range(N_HEADS):
        lo = (h % 2) * HEAD_DIM
        rows.append(acc_scr[h, lo:lo + HEAD_DIM, :] / jnp.sum(ls[h], axis=0, keepdims=True))
    o_ref[...] = jnp.concatenate(rows, axis=0).T


def _prompt_attention(pr, tbl, nb, lp, n_sel):
    nqb = lp // QB
    nkt = lp // KB
    m = nb * lp
    qblk = lambda c: pl.BlockSpec((QB, c), lambda b, i: (b * nqb + i, 0))
    kres = lambda c: pl.BlockSpec((lp, c), lambda b, i: (b, 0))
    return pl.pallas_call(
        functools.partial(_pattn_kernel, n_sel=n_sel),
        grid=(nb, nqb),
        in_specs=[qblk(256), qblk(256), qblk(256), qblk(128), kres(256), kres(256), kres(128), kres(128),
                  pl.BlockSpec((N_HEADS, KPQ + 1, KB, QB), lambda b, i: (0, 0, 0, 0))],
        out_specs=qblk(256),
        out_shape=jax.ShapeDtypeStruct((m, 256), F32),
        scratch_shapes=[pltpu.VMEM((nkt, KB, QB), I32), pltpu.VMEM((nkt, KB, QB), I16),
                        pltpu.VMEM((nkt, KB, QB), I16), pltpu.VMEM((nkt, KB, QB), F32),
                        pltpu.VMEM((nkt, D_GROUP, KB), BF16), pltpu.VMEM((N_HEADS, D_GROUP // 2, QB), F32),
                        pltpu.VMEM((N_HEADS, nkt, KB, QB), BF16), pltpu.VMEM((N_HEADS, nkt, SUB, QB), F32)],
        compiler_params=_params(("parallel", "arbitrary")),
        name="prompt_sparse_attention",
    )(pr["qa"], pr["qb"], pr["qib"], pr["wi"], pr["kb"], pr["vb"], pr["kia"], pr["kib"], tbl)


def _ssm_tail(y_lin, u, vec_ref, wglu_ref):
    y = y_lin + vec_ref[0:1, :] * u
    z = _gelu_tanh(y)
    return z * _sigmoid(_dot(z.astype(BF16), wglu_ref[...]))


def _conv_tail(acc, vec_ref):
    y = acc + vec_ref[1:2, :]
    mu = jnp.mean(y, axis=-1, keepdims=True)
    var = jnp.mean(jnp.square(y - mu), axis=-1, keepdims=True)
    y = (y - mu) * lax.rsqrt(var + EPS) * vec_ref[2:3, :] + vec_ref[3:4, :]
    return y * _sigmoid(y)


def _pool_pick(sums, cnts, cur):
    lane = lax.broadcasted_iota(I32, cur.shape, 1)
    grp = lane // (D_GROUP // len(POOL_WINDOWS))
    d = sums[3] / cnts[3]
    for g in (2, 1, 0):
        d = jnp.where(grp == g, sums[g] / cnts[g], d)
    return d - cur


def _pmix_kernel(us_ref, ca_ref, cg_ref, pp_ref, bbr_ref, bbi_ref, cre_ref, cim_ref, coef_ref, vec_ref,
                 wglu_ref, cw_ref, pw_ref,
                 yssm_ref, yconv_ref, ypool_ref, u_ref, hfin_ref,
                 hr_scr, hi_scr, car_scr, cext_scr, pext_scr, *, fin_chunk, fin_grp):
    i = pl.program_id(1)
    rows = BLK

    @pl.when(i == 0)
    def _():
        car_scr[...] = jnp.zeros_like(car_scr)
        cext_scr[0:32, :] = jnp.zeros((32, D_GROUP), F32)
        pext_scr[0:16, :] = jnp.zeros((16, D_GROUP), F32)

    us = us_ref[...]
    ub = us.astype(BF16)
    hr_scr[...] = _dot(ub, bbr_ref[...])
    hi_scr[...] = _dot(ub, bbi_ref[...])
    k1r, k1i, k2r, k2i, k4r, k4i, pwr, pwi = (coef_ref[n] for n in range(8))

    def grp_body(r, carry):
        cr, ci = carry
        sl = pl.ds(pl.multiple_of(r * SUB, SUB), SUB)
        xr = hr_scr[sl, :]
        xi = hi_scr[sl, :]
        for sh, (ar, ai) in ((1, (k1r, k1i)), (2, (k2r, k2i)), (4, (k4r, k4i))):
            sr = pltpu.roll(xr, sh, 0)
            si = pltpu.roll(xi, sh, 0)
            xr, xi = xr + ar * sr - ai * si, xi + ar * si + ai * sr
        hr = xr + pwr * cr - pwi * ci
        hi = xi + pwr * ci + pwi * cr
        hr_scr[sl, :] = hr
        hi_scr[sl, :] = hi
        return (jnp.broadcast_to(hr[SUB - 1:SUB, :], hr.shape), jnp.broadcast_to(hi[SUB - 1:SUB, :], hi.shape))

    cr, ci = lax.fori_loop(0, rows // SUB, grp_body, (car_scr[0], car_scr[1]))
    car_scr[0] = cr
    car_scr[1] = ci

    @pl.when(i == fin_chunk)
    def _():
        hfin_ref[0] = hr_scr[fin_grp * SUB:(fin_grp + 1) * SUB, :]
        hfin_ref[1] = hi_scr[fin_grp * SUB:(fin_grp + 1) * SUB, :]

    y_lin = _dot(hr_scr[...].astype(BF16), cre_ref[...]) - _dot(hi_scr[...].astype(BF16), cim_ref[...])
    yssm_ref[...] = _ssm_tail(y_lin, us, vec_ref, wglu_ref)

    u = ca_ref[...] * _sigmoid(cg_ref[...])
    u_ref[...] = u
    cext_scr[32:32 + rows, :] = u
    acc = jnp.zeros((rows, D_GROUP), F32)
    for k in range(CONV_WIDTH):
        acc = acc + cw_ref[k:k + 1, :] * cext_scr[2 + k:2 + k + rows, :]
    yconv_ref[...] = _conv_tail(acc, vec_ref)
    cext_scr[0:32, :] = cext_scr[rows:rows + 32, :]

    p = pp_ref[...]
    pext_scr[16:16 + rows, :] = p
    e = pext_scr[...]
    s2 = e + pltpu.roll(e, 1, 0)
    s4 = s2 + pltpu.roll(s2, 2, 0)
    s8 = s4 + pltpu.roll(s4, 4, 0)
    s16 = s8 + pltpu.roll(s8, 8, 0)
    pos1 = (i * rows + 1 + lax.broadcasted_iota(I32, (rows, D_GROUP), 0)).astype(F32)
    sums = [s[16:16 + rows, :] for s in (s2, s4, s8, s16)]
    cnts = [jnp.minimum(pos1, float(w)) for w in POOL_WINDOWS]
    d = _pool_pick(sums, cnts, p)
    ypool_ref[...] = _dot(d.astype(BF16), pw_ref[...]) * vec_ref[4:5, :]
    pext_scr[0:16, :] = pext_scr[rows:rows + 16, :]


def _prompt_mixers(pr, sp, nb, lp, t_real):
    nqb = lp // BLK
    m = nb * lp
    blk = pl.BlockSpec((BLK, D_GROUP), lambda b, i: (b * nqb + i, 0))
    full = lambda a: pl.BlockSpec(a.shape, lambda b, i: (0,) * a.ndim)
    last = t_real - 1
    weights = [sp["bbr"], sp["bbi"], sp["cre"], sp["cim"], sp["coef"], sp["vec"], sp["wglu"], sp["cw"], sp["pw"]]
    f256 = jax.ShapeDtypeStruct((m, D_GROUP), F32)
    return pl.pallas_call(
        functools.partial(_pmix_kernel, fin_chunk=last // BLK, fin_grp=(last % BLK) // SUB),
        grid=(nb, nqb),
        in_specs=[blk, blk, blk, blk] + [full(w) for w in weights],
        out_specs=[blk, blk, blk, blk, pl.BlockSpec((None, 2, SUB, SSM_LANES), lambda b, i: (b, 0, 0, 0))],
        out_shape=[f256, f256, f256, f256, jax.ShapeDtypeStruct((nb, 2, SUB, SSM_LANES), F32)],
        scratch_shapes=[pltpu.VMEM((BLK, SSM_LANES), F32), pltpu.VMEM((BLK, SSM_LANES), F32),
                        pltpu.VMEM((2, SUB, SSM_LANES), F32), pltpu.VMEM((32 + BLK, D_GROUP), F32),
                        pltpu.VMEM((16 + BLK, D_GROUP), F32)],
        compiler_params=_params(("parallel", "arbitrary")),
        name="prompt_ssm_conv_pool",
    )(pr["us"], pr["ca"], pr["cg"], pr["pp"], *weights)


def _smix_kernel(us_ref, ca_ref, cg_ref, pp_ref, h0r_ref, h0i_ref, cst_ref, pst_ref,
                 bbr_ref, bbi_ref, cre_ref, cim_ref, ab_ref, vec_ref, wglu_ref, cw_ref, pw_ref,
                 yssm_ref, yconv_ref, ypool_ref, u_ref, hr_ref, hi_ref, *, t_new):
    c = D_GROUP
    sl = lambda t: slice(t * c, (t + 1) * c)
    ar = ab_ref[0:1, :]
    ai = ab_ref[1:2, :]
    hr = h0r_ref[...]
    hi = h0i_ref[...]
    for t in range(t_new):
        us = us_ref[:, sl(t)]
        ub = us.astype(BF16)
        hr, hi = ar * hr - ai * hi + _dot(ub, bbr_ref[...]), ar * hi + ai * hr + _dot(ub, bbi_ref[...])
        y_lin = _dot(hr.astype(BF16), cre_ref[...]) - _dot(hi.astype(BF16), cim_ref[...])
        yssm_ref[:, sl(t)] = _ssm_tail(y_lin, us, vec_ref, wglu_ref)
    hr_ref[...] = hr
    hi_ref[...] = hi

    n_cst = CONV_WIDTH - 1
    ext = [cst_ref[:, sl(j)] for j in range(n_cst)]
    for t in range(t_new):
        u = ca_ref[:, sl(t)] * _sigmoid(cg_ref[:, sl(t)])
        u_ref[:, sl(t)] = u
        ext.append(u)
    for t in range(t_new):
        acc = cw_ref[0:1, :] * ext[t]
        for k in range(1, CONV_WIDTH):
            acc = acc + cw_ref[k:k + 1, :] * ext[t + k]
        yconv_ref[:, sl(t)] = _conv_tail(acc, vec_ref)

    pe = [pst_ref[:, sl(j)] for j in range(POOL_BUF)] + [pp_ref[:, sl(t)] for t in range(t_new)]
    n = len(pe)
    s2 = [None] + [pe[j] + pe[j - 1] for j in range(1, n)]
    s4 = [None] * 3 + [s2[j] + s2[j - 2] for j in range(3, n)]
    s8 = [None] * 7 + [s4[j] + s4[j - 4] for j in range(7, n)]
    s16 = [None] * 15 + [s8[j] + s8[j - 8] for j in range(15, n)]
    for t in range(t_new):
        j = POOL_BUF + t
        cnts = [float(w) for w in POOL_WINDOWS]
        d = _pool_pick([s2[j], s4[j], s8[j], s16[j]], cnts, pe[j])
        ypool_ref[:, sl(t)] = _dot(d.astype(BF16), pw_ref[...]) * vec_ref[4:5, :]


def _sample_mixers(sm, sp, h0r, h0i, cst, pst, nseq, t_new):
    wide = lambda a: a.reshape(nseq, t_new * D_GROUP)
    args = [wide(sm["us"]), wide(sm["ca"]), wide(sm["cg"]), wide(sm["pp"]), h0r, h0i, cst, pst,
            sp["bbr"], sp["bbi"], sp["cre"], sp["cim"], sp["ab"], sp["vec"], sp["wglu"], sp["cw"], sp["pw"]]
    full = lambda a: pl.BlockSpec(a.shape, lambda i: (0,) * a.ndim)
    fw = jax.ShapeDtypeStruct((nseq, t_new * D_GROUP), F32)
    fs = jax.ShapeDtypeStruct((nseq, SSM_LANES), F32)
    shapes = [fw, fw, fw, fw, fs, fs]
    return pl.pallas_call(
        functools.partial(_smix_kernel, t_new=t_new),
        grid=(1,),
        in_specs=[full(a) for a in args],
        out_specs=[full(s) for s in shapes],
        out_shape=shapes,
        compiler_params=_params(("arbitrary",)),
        name="sample_ssm_conv_pool",
    )(*args)


def _sscore_kernel(pt_ref, qib_ref, wi_ref, kin_ref, *rest, n_pages, t_new):
    page_refs = rest[:n_pages]
    s_ref = rest[n_pages]
    qif = qib_ref[...].astype(F32)
    qst = jnp.concatenate([qif[:, h * IDX_DIM:(h + 1) * IDX_DIM] for h in range(IDX_HEADS)], axis=0).astype(BF16)
    wcol = jnp.concatenate([wi_ref[:, h:h + 1] for h in range(IDX_HEADS)], axis=0)

    def score(qk):
        d = wcol * jnp.maximum(qk, 0.0)
        s = d[0:t_new]
        for h in range(1, IDX_HEADS):
            s = s + d[h * t_new:(h + 1) * t_new]
        return s

    for p in range(n_pages):
        s_ref[:, p * PAGE:(p + 1) * PAGE] = score(_dot(qst, page_refs[p][...].astype(BF16)))
    knew = jnp.concatenate([kin_ref[...].astype(F32)[:, 0:IDX_DIM], jnp.zeros((PAGE - t_new, IDX_DIM), F32)], axis=0)
    s = score(_dot_nt(qst, knew.astype(BF16)))
    r = lax.broadcasted_iota(I32, s.shape, 0)
    c = lax.broadcasted_iota(I32, s.shape, 1)
    s_ref[:, n_pages * PAGE:(n_pages + 1) * PAGE] = jnp.where(c <= r, s, -jnp.inf)


def _sample_scores(sm, cache_kidx, pt_flat, layer, nseq, t_new, n_pages):
    ncol = (n_pages + 1) * PAGE
    row = lambda c: pl.BlockSpec((t_new, c), lambda b, pt: (b, 0))
    page = lambda p: pl.BlockSpec((None, None, IDX_DIM, PAGE),
                                  lambda b, pt, p=p: (layer, pt[b * n_pages + p], 0, 0))
    grid_spec = pltpu.PrefetchScalarGridSpec(
        num_scalar_prefetch=1, grid=(nseq,),
        in_specs=[row(256), row(128), row(128)] + [page(p) for p in range(n_pages)],
        out_specs=pl.BlockSpec((None, t_new, ncol), lambda b, pt: (b, 0, 0)))
    return pl.pallas_call(
        functools.partial(_sscore_kernel, n_pages=n_pages, t_new=t_new),
        grid_spec=grid_spec,
        out_shape=jax.ShapeDtypeStruct((nseq, t_new, ncol), F32),
        compiler_params=_params(("parallel",)),
        name="sample_indexer_scores",
    )(pt_flat, sm["qib"], sm["wi"], sm["kia"], *([cache_kidx] * n_pages))


def _sselect_kernel(s_ref, am_ref, ks_scr, hi_scr, lo_scr, am_scr, *, nkb, n_sel):
    for j in range(nkb):
        key = _sortable_key(s_ref[j * KB:(j + 1) * KB, :])
        ks_scr[j] = key
        hi_scr[j] = (key >> 16).astype(I16)
        lo_scr[j] = ((key & 0xFFFF) - HALF_BIAS).astype(I16)
    _select_mask_t(ks_scr, hi_scr, lo_scr, am_scr, nkb, n_sel)
    for j in range(nkb):
        am_ref[j * KB:(j + 1) * KB, :] = am_scr[j]


def _sample_select(scores, n_sel):
    m, ncol = scores.shape
    nkb = -(-ncol // KB)
    st = jnp.concatenate([scores, jnp.full((m, nkb * KB - ncol), -jnp.inf, F32)], axis=1).T
    spec = pl.BlockSpec((nkb * KB, QB), lambda i: (0, i))
    blk = lambda dt: pltpu.VMEM((nkb, KB, QB), dt)
    amt = pl.pallas_call(
        functools.partial(_sselect_kernel, nkb=nkb, n_sel=n_sel),
        grid=(m // QB,),
        in_specs=[spec],
        out_specs=spec,
        out_shape=jax.ShapeDtypeStruct((nkb * KB, m), F32),
        scratch_shapes=[blk(I32), blk(I16), blk(I16), blk(F32)],
        compiler_params=_params(("parallel",)),
        name="sample_topn_mask",
    )(st)
    return amt.T


def _sattn_kernel(pt_ref, q_ref, am_ref, kn_ref, vn_ref, tbl_ref, *rest, n_pages, t_new):
    k_refs = rest[:n_pages]
    v_refs = rest[n_pages:2 * n_pages]
    o_ref = rest[2 * n_pages]
    q = q_ref[...].astype(F32)
    lane = lax.broadcasted_iota(I32, q.shape, 1)
    qbd = jnp.concatenate([jnp.where(lane // HEAD_DIM == h, q, 0.0) for h in range(N_HEADS)], axis=0).astype(BF16)
    zpad = jnp.zeros((PAGE - t_new, D_GROUP), F32)
    pad_page = lambda r: jnp.concatenate([r[...].astype(F32), zpad], axis=0).astype(BF16)
    lgs = []
    for p in range(n_pages + 1):
        am = am_ref[:, p * PAGE:(p + 1) * PAGE]
        am4 = jnp.concatenate([am] * N_HEADS, axis=0)
        tsel = 0 if p < n_pages - 1 else (1 if p == n_pages - 1 else 2)
        qk = _dot(qbd, k_refs[p][...].astype(BF16)) if p < n_pages else _dot_nt(qbd, pad_page(kn_ref))
        lgs.append(qk + tbl_ref[tsel] + am4)
    mx = lgs[0]
    for lg in lgs[1:]:
        mx = jnp.maximum(mx, lg)
    mx = jnp.max(mx, axis=-1, keepdims=True)
    ps = [jnp.exp(lg - mx) for lg in lgs]
    tot = ps[0]
    for pp_ in ps[1:]:
        tot = tot + pp_
    den = jnp.sum(tot, axis=-1, keepdims=True)
    acc = _dot(ps[n_pages].astype(BF16), pad_page(vn_ref))
    for p in range(n_pages):
        acc = acc + _dot_nt(ps[p].astype(BF16), v_refs[p][...].astype(BF16))
    acc = acc / den
    out = acc[0:t_new]
    for h in range(1, N_HEADS):
        out = jnp.where(lane // HEAD_DIM == h, acc[h * t_new:(h + 1) * t_new], out)
    o_ref[...] = out


def _sample_attention(sm, qs, am, tbl, cache_k, cache_v, pt_flat, layer, nseq, t_new, n_pages):
    row = lambda c: pl.BlockSpec((t_new, c), lambda b, pt: (b, 0))
    page = lambda p: pl.BlockSpec((None, None, D_GROUP, PAGE),
                                  lambda b, pt, p=p: (layer, pt[b * n_pages + p], 0, 0))
    grid_spec = pltpu.PrefetchScalarGridSpec(
        num_scalar_prefetch=1, grid=(nseq,),
        in_specs=[row(256), row(am.shape[1]), row(256), row(256),
                  pl.BlockSpec(tbl.shape, lambda b, pt: (0, 0, 0))]
        + [page(p) for p in range(n_pages)] + [page(p) for p in range(n_pages)],
        out_specs=row(256))
    return pl.pallas_call(
        functools.partial(_sattn_kernel, n_pages=n_pages, t_new=t_new),
        grid_spec=grid_spec,
        out_shape=jax.ShapeDtypeStruct((nseq * t_new, D_GROUP), F32),
        compiler_params=_params(("parallel",)),
        name="sample_sparse_attention",
    )(pt_flat, qs, am, sm["kb"], sm["vb"], tbl, *([cache_k] * n_pages), *([cache_v] * n_pages))


def _outproj_kernel(x_ref, a_ref, s_ref, c_ref, p_ref, w_ref, o_ref):
    acc = x_ref[...]
    for n, r in enumerate((a_ref, s_ref, c_ref, p_ref)):
        acc = acc + _dot(r[...].astype(BF16), w_ref[n * D_GROUP:(n + 1) * D_GROUP, :])
    o_ref[...] = acc


def _outproj(x, ys, w, tm):
    m, d = x.shape
    row = lambda c: pl.BlockSpec((tm, c), lambda i: (i, 0))
    return pl.pallas_call(
        _outproj_kernel,
        grid=(m // tm,),
        in_specs=[row(d)] + [row(D_GROUP)] * 4 + [pl.BlockSpec(w.shape, lambda i: (0, 0))],
        out_specs=row(d),
        out_shape=jax.ShapeDtypeStruct((m, d), F32),
        compiler_params=_params(("parallel",)),
        name="mixer_out_proj",
    )(x, *ys, w)


def _rmsnorm_kernel(x_ref, g_ref, o_ref):
    x = x_ref[...]
    ms = jnp.mean(x * x, axis=-1, keepdims=True)
    o_ref[...] = x * lax.rsqrt(ms + EPS) * g_ref[...]


def _rmsnorm(x, g, tm):
    m, d = x.shape
    return pl.pallas_call(
        _rmsnorm_kernel,
        grid=(m // tm,),
        in_specs=[pl.BlockSpec((tm, d), lambda i: (i, 0)), pl.BlockSpec((1, d), lambda i: (0, 0))],
        out_specs=pl.BlockSpec((tm, d), lambda i: (i, 0)),
        out_shape=jax.ShapeDtypeStruct((m, d), F32),
        compiler_params=_params(("parallel",)),
        name="final_rmsnorm",
    )(x, g)


def _bucket_table(dist):
    n = np.maximum(dist, 0)
    max_exact = N_BUCKETS // 2
    nf = np.maximum(n, 1).astype(np.float32)
    large = max_exact + (np.log(nf / np.float32(max_exact)) / np.float32(math.log(MAX_DISTANCE / max_exact))
                         * np.float32(N_BUCKETS - max_exact)).astype(np.int32)
    large = np.minimum(large, N_BUCKETS - 1)
    return np.where(n < max_exact, n, large).astype(np.int32)


def _bias_lookup_kernel(rb_ref, idx_ref, o_ref, *, shift_bucket):
    idx = idx_ref[...]

    def body(b, accs):
        hit = idx == b
        return tuple(jnp.where(hit, rb_ref[b * N_HEADS + h], accs[h]) for h in range(N_HEADS))

    accs = lax.fori_loop(0, N_BUCKETS, body, tuple(jnp.zeros(idx.shape, F32) for _ in range(N_HEADS)))
    for h in range(N_HEADS):
        shift = 0.0 if shift_bucket is None else rb_ref[shift_bucket * N_HEADS + h]
        o_ref[h] = accs[h] - shift


def _bias_lookup(rel_bias, idx, shift_bucket):
    nt, r, c = idx.shape
    return pl.pallas_call(
        functools.partial(_bias_lookup_kernel, shift_bucket=shift_bucket),
        grid=(nt,),
        in_specs=[pl.BlockSpec(memory_space=pltpu.SMEM), pl.BlockSpec((None, r, c), lambda t: (t, 0, 0))],
        out_specs=pl.BlockSpec((N_HEADS, None, r, c), lambda t: (0, t, 0, 0)),
        out_shape=jax.ShapeDtypeStruct((N_HEADS, nt, r, c), F32),
        compiler_params=_params(("parallel",)),
        name="rel_bias_tables",
    )(rel_bias.reshape(-1).astype(F32), jnp.asarray(idx))


def _bias_tables(rel_bias, t_new):
    kk = np.arange(KB)[:, None]
    qq = np.arange(QB)[None, :]
    pidx = np.stack([_bucket_table((t - (KPQ - 1)) * KB + qq - kk) for t in range(KPQ + 1)])
    far_bucket = int(_bucket_table(np.array([2 * KB + 1]))[0])
    ptbl = _bias_lookup(rel_bias, pidx, far_bucket)
    c = np.arange(BLK)[None, :]
    t = np.arange(t_new)[:, None]
    sidx = np.stack([_bucket_table(np.full((t_new, BLK), 2 * BLK)), _bucket_table(BLK + t - c),
                     _bucket_table(t - c)])
    stbl = jnp.transpose(_bias_lookup(rel_bias, sidx, None), (1, 0, 2, 3)).reshape(3, N_HEADS * t_new, BLK)
    return ptbl, stbl


def _cmul(a, b):
    return a[0] * b[0] - a[1] * b[1], a[0] * b[1] + a[1] * b[0]


def _ssm_params(a_re, a_im, log_dt, b_re, b_im, c_re, c_im):
    nl = a_re.shape[0]
    dt = jnp.exp(log_dt)[..., None]
    decay = jnp.exp(dt * a_re)
    ab_re = decay * jnp.cos(dt * a_im)
    ab_im = decay * jnp.sin(dt * a_im)
    den = a_re * a_re + a_im * a_im
    f_re = ((ab_re - 1.0) * a_re + ab_im * a_im) / den
    f_im = (ab_im * a_re - (ab_re - 1.0) * a_im) / den
    bb_re = f_re[..., None] * b_re - f_im[..., None] * b_im
    bb_im = f_re[..., None] * b_im + f_im[..., None] * b_re
    eye = jnp.eye(SSM_GROUPS, dtype=F32)
    pack_b = lambda b: jnp.einsum("lgpc,gh->lgchp", b, eye).reshape(nl, D_GROUP, SSM_LANES).astype(BF16)
    pack_c = lambda c: jnp.einsum("lgcp,gh->lgphc", c, eye).reshape(nl, SSM_LANES, D_GROUP).astype(BF16)
    a1 = (ab_re.reshape(nl, SSM_LANES), ab_im.reshape(nl, SSM_LANES))
    a2 = _cmul(a1, a1)
    a3 = _cmul(a2, a1)
    a4 = _cmul(a2, a2)
    a5 = _cmul(a4, a1)
    a6 = _cmul(a4, a2)
    a7 = _cmul(a4, a3)
    a8 = _cmul(a4, a4)
    t = jnp.arange(SUB)[None, :, None]
    gate = lambda a, s: jnp.where(t >= s, a[:, None, :], 0.0)
    pw_re = jnp.stack([a[0] for a in (a1, a2, a3, a4, a5, a6, a7, a8)], axis=1)
    pw_im = jnp.stack([a[1] for a in (a1, a2, a3, a4, a5, a6, a7, a8)], axis=1)
    coef = jnp.stack([gate(a1[0], 1), gate(a1[1], 1), gate(a2[0], 2), gate(a2[1], 2),
                      gate(a4[0], 4), gate(a4[1], 4), pw_re, pw_im], axis=1)
    ab = jnp.stack([a1[0], a1[1]], axis=1)
    return pack_b(bb_re), pack_b(bb_im), pack_c(c_re), pack_c(c_im), coef, ab


def _pad_rows(a, n):
    return jnp.concatenate([a, jnp.zeros((n - a.shape[0],) + a.shape[1:], a.dtype)], axis=0)


def kernel(x_prompt, x_sample, cache_k, cache_v, cache_kidx, page_table, state_ssm_re, state_ssm_im, state_conv, state_pool, meta_tokens, rel_bias, norm_ffn1, ffn1_w_gate, ffn1_w_up, ffn1_w_down, norm_mix, w_in, w_out, ssm_a_re, ssm_a_im, ssm_log_dt, ssm_b_re, ssm_b_im, ssm_c_re, ssm_c_im, ssm_d, ssm_w_glu, conv_w, conv_b, conv_ln_g, conv_ln_b, pool_w, pool_scale, norm_ffn2, ffn2_w_gate, ffn2_w_up, ffn2_w_down, norm_final):
    nb, seq, d = x_prompt.shape
    nseq, t_new, _ = x_sample.shape
    depth = w_in.shape[0]
    n_pages = page_table.shape[1]
    n_pool = cache_k.shape[1]
    t_real = seq + N_META
    lp = -(-t_real // QB) * QB
    past = n_pages * PAGE
    n_sel_p = min(TOPK_MAX, t_real // 4)
    n_sel_s = min(TOPK_MAX, (past + t_new) // 4)
    mp, ms = nb * lp, nseq * t_new
    tm_p, tm_s = 1024, 512
    assert mp % tm_p == 0 and ms % tm_s == 0 and t_new == SUB

    meta = jnp.broadcast_to(meta_tokens[None].astype(F32), (nb, N_META, d))
    xp = jnp.concatenate([meta, x_prompt, jnp.zeros((nb, lp - t_real, d), F32)], axis=1).reshape(mp, d)
    xs = x_sample.reshape(ms, d)

    zc = lambda n: jnp.zeros((depth, d, n), F32)
    q_, k_, v_, qi_, ki_, wi_, rest = (w_in[..., 0:256], w_in[..., 256:512], w_in[..., 512:768], w_in[..., 768:1024],
                                       w_in[..., 1024:1088], w_in[..., 1088:1092], w_in[..., 1092:])
    w_in_p = jnp.concatenate([q_, k_, v_, qi_, ki_, zc(64), zc(64), ki_, wi_, zc(124), rest], axis=-1).astype(BF16)
    w_out_b = w_out.astype(BF16)
    ffn_w = [[w.astype(BF16) for w in ws] for ws in ((ffn1_w_gate, ffn1_w_up, ffn1_w_down),
                                                     (ffn2_w_gate, ffn2_w_up, ffn2_w_down))]
    bbr, bbi, cre, cim, coef, ab = _ssm_params(ssm_a_re, ssm_a_im, ssm_log_dt, ssm_b_re, ssm_b_im, ssm_c_re, ssm_c_im)
    zrow = jnp.zeros((depth, D_GROUP), F32)
    vec = jnp.stack([ssm_d, conv_b, conv_ln_g, conv_ln_b, pool_scale, zrow, zrow, zrow], axis=1)
    cw = jnp.concatenate([conv_w, jnp.zeros((depth, 1, D_GROUP), F32)], axis=1)
    ng = len(POOL_WINDOWS)
    pw = jnp.einsum("lgcd,gh->lgchd", pool_w, jnp.eye(ng, dtype=F32)).reshape(depth, D_GROUP, D_GROUP).astype(BF16)
    wglu = ssm_w_glu.astype(BF16)
    ptbl, stbl = _bias_tables(rel_bias, t_new)

    ck = jnp.transpose(cache_k, (0, 1, 3, 4, 2)).reshape(depth, n_pool, D_GROUP, PAGE)
    cv = jnp.transpose(cache_v, (0, 1, 3, 4, 2)).reshape(depth, n_pool, D_GROUP, PAGE)
    cki = jnp.transpose(cache_kidx, (0, 1, 3, 2))
    pt_flat = page_table.reshape(-1).astype(I32)
    h0r = state_ssm_re.reshape(depth, nseq, SSM_LANES)
    h0i = state_ssm_im.reshape(depth, nseq, SSM_LANES)
    cst = state_conv.reshape(depth, nseq, (CONV_WIDTH - 1) * D_GROUP)
    pst = state_pool.reshape(depth, nseq, POOL_BUF * D_GROUP)
    names = ("k", "v", "ki", "wi", "qa", "qb", "kb", "vb", "qib", "kia", "kib", "us", "ca", "cg", "pp")

    pst_out = [[] for _ in range(7)]
    sst_out = [[] for _ in range(7)]
    for l in range(depth):
        g1 = norm_ffn1[l][None]
        xp = _ffn(xp, g1, ffn_w[0][0][l], ffn_w[0][1][l], ffn_w[0][2][l], tm_p)
        xs = _ffn(xs, g1, ffn_w[0][0][l], ffn_w[0][1][l], ffn_w[0][2][l], tm_s)

        gm = norm_mix[l][None]
        pr = dict(zip(names, _proj(xp, gm, w_in_p[l], tm_p // 2)))
        sm = dict(zip(names, _proj(xs, gm, w_in_p[l], tm_s)))
        sp = dict(bbr=bbr[l], bbi=bbi[l], cre=cre[l], cim=cim[l], coef=coef[l], ab=ab[l], vec=vec[l],
                  wglu=wglu[l], cw=cw[l], pw=pw[l])

        ya_p = _prompt_attention(pr, ptbl, nb, lp, n_sel_p)
        ys_p, yc_p, yp_p, u_p, hfin = _prompt_mixers(pr, sp, nb, lp, t_real)
        xp = _outproj(xp, (ya_p, ys_p, yc_p, yp_p), w_out_b[l], tm_p)

        scores = _sample_scores(sm, cki, pt_flat, l, nseq, t_new, n_pages)
        am = _sample_select(scores.reshape(ms, -1), n_sel_s)
        qs = sm["qa"] + sm["qb"]
        ya_s = _sample_attention(sm, qs, am, stbl, ck, cv, pt_flat, l, nseq, t_new, n_pages)
        ys_s, yc_s, yp_s, u_s, hr_s, hi_s = _sample_mixers(sm, sp, h0r[l], h0i[l], cst[l], pst[l], nseq, t_new)
        wide = lambda a: a.reshape(ms, D_GROUP)
        xs = _outproj(xs, (ya_s, wide(ys_s), wide(yc_s), wide(yp_s)), w_out_b[l], tm_s)

        g2 = norm_ffn2[l][None]
        xp = _ffn(xp, g2, ffn_w[1][0][l], ffn_w[1][1][l], ffn_w[1][2][l], tm_p)
        xs = _ffn(xs, g2, ffn_w[1][0][l], ffn_w[1][1][l], ffn_w[1][2][l], tm_s)

        seq3 = lambda a: a.reshape(nb, lp, -1)[:, :t_real]
        fin_row = (t_real - 1) % SUB
        pst_out[0].append(seq3(pr["k"]).reshape(nb, t_real, N_HEADS, HEAD_DIM))
        pst_out[1].append(seq3(pr["v"]).reshape(nb, t_real, N_HEADS, HEAD_DIM))
        pst_out[2].append(seq3(pr["ki"])[..., :IDX_DIM])
        pst_out[3].append(hfin[:, 0, fin_row].reshape(nb, SSM_GROUPS, SSM_STATE))
        pst_out[4].append(hfin[:, 1, fin_row].reshape(nb, SSM_GROUPS, SSM_STATE))
        pst_out[5].append(seq3(u_p)[:, t_real - (CONV_WIDTH - 1):])
        pst_out[6].append(seq3(pr["pp"])[:, t_real - POOL_BUF:])
        sst_out[0].append(sm["k"].reshape(nseq, t_new, N_HEADS, HEAD_DIM))
        sst_out[1].append(sm["v"].reshape(nseq, t_new, N_HEADS, HEAD_DIM))
        sst_out[2].append(sm["ki"][:, :IDX_DIM].reshape(nseq, t_new, IDX_DIM))
        sst_out[3].append(hr_s.reshape(nseq, SSM_GROUPS, SSM_STATE))
        sst_out[4].append(hi_s.reshape(nseq, SSM_GROUPS, SSM_STATE))
        sst_out[5].append(jnp.concatenate([state_conv[l][:, t_new:], u_s.reshape(nseq, t_new, D_GROUP)], axis=1))
        sst_out[6].append(jnp.concatenate([state_pool[l][:, t_new:], sm["pp"].reshape(nseq, t_new, D_GROUP)], axis=1))

    gf = norm_final[None]
    y_prompt = _rmsnorm(xp, gf, tm_p).reshape(nb, lp, d)[:, N_META:t_real]
    y_sample = _rmsnorm(xs, gf, tm_s).reshape(nseq, t_new, d)
    return (y_prompt, y_sample, *[jnp.stack(a) for a in pst_out], *[jnp.stack(a) for a in sst_out])
```

```python
import functools
import math

import numpy as np
import jax
import jax.numpy as jnp
from jax import lax
from jax.experimental import pallas as pl
from jax.experimental.pallas import tpu as pltpu

F32 = jnp.float32
BF16 = jnp.bfloat16
I32 = jnp.int32
I16 = jnp.int16

N_META = 16
N_HEADS = 4
HEAD_DIM = 64
IDX_HEADS = 4
IDX_DIM = 64
D_GROUP = 256
TOPK_MAX = 256
N_BUCKETS = 32
MAX_DISTANCE = 128
PAGE = 128
SSM_GROUPS = 16
SSM_GROUP_CH = 16
SSM_STATE = 64
SSM_LANES = SSM_GROUPS * SSM_STATE
CONV_WIDTH = 31
POOL_WINDOWS = (2, 4, 8, 16)
POOL_BUF = 15
EPS = 1e-6

BLK = 128
SUB = 8
QB = 256
KB = 256
KPQ = QB // KB
HALF_ROWS = 16
HALF_BIAS = 32768
NEG = -1e30
KEY_NEG_INF = -2139095041
VMEM_LIMIT = 56 * 1024 * 1024

PROJ_COLS = 2432


def _dot(a, b):
    return jnp.dot(a, b, preferred_element_type=F32)


def _dot_nt(a, b):
    return lax.dot_general(a, b, (((1,), (1,)), ((), ())), preferred_element_type=F32)


def _rms_bf16(x, g):
    ms = jnp.mean(x * x, axis=-1, keepdims=True)
    return (x * lax.rsqrt(ms + EPS) * g).astype(BF16)


def _sigmoid(x):
    return 1.0 / (1.0 + jnp.exp(-x))


def _gelu_tanh(x):
    c = math.sqrt(2.0 / math.pi)
    return 0.5 * x * (1.0 + jnp.tanh(c * (x + 0.044715 * (x * x * x))))


def _params(sem):
    return pltpu.CompilerParams(dimension_semantics=sem, vmem_limit_bytes=VMEM_LIMIT)


def _ffn_kernel(x_ref, g_ref, wg_ref, wu_ref, wd_ref, o_ref, h_scr, acc_scr):
    j = pl.program_id(1)

    @pl.when(j == 0)
    def _():
        h_scr[...] = _rms_bf16(x_ref[...], g_ref[...])
        acc_scr[...] = jnp.zeros_like(acc_scr)

    h = h_scr[...]
    g = _dot(h, wg_ref[...].astype(BF16))
    u = _dot(h, wu_ref[...].astype(BF16))
    a = (g * _sigmoid(g) * u).astype(BF16)
    acc_scr[...] += _dot(a, wd_ref[...].astype(BF16))

    @pl.when(j == pl.num_programs(1) - 1)
    def _():
        o_ref[...] = x_ref[...] + 0.5 * acc_scr[...]


def _ffn(x, g, wg, wu, wd, layer, tm, tf=512):
    m, d = x.shape
    dff = wg.shape[2]
    return pl.pallas_call(
        _ffn_kernel,
        grid=(m // tm, dff // tf),
        in_specs=[
            pl.BlockSpec((tm, d), lambda i, j: (i, 0)),
            pl.BlockSpec((None, 1, d), lambda i, j: (layer, 0, 0)),
            pl.BlockSpec((None, d, tf), lambda i, j: (layer, 0, j)),
            pl.BlockSpec((None, d, tf), lambda i, j: (layer, 0, j)),
            pl.BlockSpec((None, tf, d), lambda i, j: (layer, j, 0)),
        ],
        out_specs=pl.BlockSpec((tm, d), lambda i, j: (i, 0)),
        out_shape=jax.ShapeDtypeStruct((m, d), F32),
        scratch_shapes=[pltpu.VMEM((tm, d), BF16), pltpu.VMEM((tm, d), F32)],
        compiler_params=_params(("parallel", "arbitrary")),
        name="half_ffn",
    )(x, g, wg, wu, wd)


def _proj_kernel(x_ref, g_ref, w_ref, k_ref, v_ref, ki_ref, wi_ref, qa_ref, qb_ref, kb_ref, vb_ref,
                 qib_ref, kia_ref, kib_ref, us_ref, ca_ref, cg_ref, pp_ref):
    h = _rms_bf16(x_ref[...], g_ref[...])
    z = _dot(h, w_ref[...])
    q = z[:, 0:256] * (HEAD_DIM ** -0.5)
    lane = lax.broadcasted_iota(I32, q.shape, 1)
    even = ((lane // HEAD_DIM) % 2) == 0
    qa_ref[...] = jnp.where(even, q, 0.0).astype(BF16)
    qb_ref[...] = jnp.where(even, 0.0, q).astype(BF16)
    k = z[:, 256:512]
    v = z[:, 512:768]
    k_ref[...] = k
    v_ref[...] = v
    kb_ref[...] = k.astype(BF16)
    vb_ref[...] = v.astype(BF16)
    qib_ref[...] = z[:, 768:1024].astype(BF16)
    kia = z[:, 1024:1152]
    ki_ref[...] = kia
    kia_ref[...] = kia.astype(BF16)
    kib_ref[...] = z[:, 1152:1280].astype(BF16)
    wi_ref[...] = z[:, 1280:1408]
    us_ref[...] = z[:, 1408:1664]
    ca_ref[...] = z[:, 1664:1920]
    cg_ref[...] = z[:, 1920:2176]
    pp_ref[...] = z[:, 2176:2432]


def _proj(x, g, w, layer, tm):
    m, d = x.shape
    row = lambda i: (i, 0)
    f256 = jax.ShapeDtypeStruct((m, 256), F32)
    f128 = jax.ShapeDtypeStruct((m, 128), F32)
    b256 = jax.ShapeDtypeStruct((m, 256), BF16)
    b128 = jax.ShapeDtypeStruct((m, 128), BF16)
    shapes = [f256, f256, f128, f128, b256, b256, b256, b256, b256, b128, b128, f256, f256, f256, f256]
    return pl.pallas_call(
        _proj_kernel,
        grid=(m // tm,),
        in_specs=[
            pl.BlockSpec((tm, d), row),
            pl.BlockSpec((None, 1, d), lambda i: (layer, 0, 0)),
            pl.BlockSpec((None, d, PROJ_COLS), lambda i: (layer, 0, 0)),
        ],
        out_specs=[pl.BlockSpec((tm, s.shape[1]), row) for s in shapes],
        out_shape=shapes,
        compiler_params=_params(("parallel",)),
        name="mixer_in_proj",
    )(x, g, w)


def _sortable_key(s):
    b = lax.bitcast_convert_type(s, I32)
    k = b ^ ((b >> 31) & 0x7FFFFFFF)
    return jnp.where(k == -1, 0, k)


def _fold(x, op):
    x3 = x.reshape(x.shape[0] // SUB, SUB, x.shape[1])
    return op(x3, axis=0)


def _tree_sum(parts):
    while len(parts) > 1:
        parts = [parts[n] + parts[n + 1] for n in range(0, len(parts) - 1, 2)] + parts[len(parts) & ~1:]
    return parts[0]


def _count16(ref, nkb, ref_val, strict):
    nq = ref.shape[2]
    cs = jnp.broadcast_to(ref_val, (HALF_ROWS, nq)).astype(I16)[None]
    one, zero = jnp.int16(1), jnp.int16(0)

    def blk(j, pc):
        x3 = ref[j].reshape(KB // HALF_ROWS, HALF_ROWS, nq)
        m3 = jnp.where(x3 > cs if strict else x3 >= cs, one, zero)
        return pc + _tree_sum([m3[g] for g in range(KB // HALF_ROWS)])

    pc = lax.fori_loop(0, nkb, blk, jnp.zeros((HALF_ROWS, nq), I16))
    return jnp.sum(pc.astype(I32), axis=0, keepdims=True)


def _nth_largest16(ref, nkb, n_need):
    nq = ref.shape[2]

    def bit_body(bi, t):
        c = t | jnp.left_shift(jnp.int32(1), 15 - bi)
        return jnp.where(_count16(ref, nkb, c - HALF_BIAS, False) >= n_need, c, t)

    return lax.fori_loop(0, 16, bit_body, jnp.zeros((1, nq), I32)) - HALF_BIAS


def _select_mask_t(ks_ref, hi_ref, lo_ref, am_ref, nkb, n_sel):
    nq = ks_ref.shape[2]
    t_hi = _nth_largest16(hi_ref, nkb, n_sel)
    n_above = _count16(hi_ref, nkb, t_hi, True)
    t_hi16 = jnp.broadcast_to(t_hi, (HALF_ROWS, nq)).astype(I16)[None]

    def park(j, carry):
        shape3 = (KB // HALF_ROWS, HALF_ROWS, nq)
        lo3 = jnp.where(hi_ref[j].reshape(shape3) == t_hi16, lo_ref[j].reshape(shape3), jnp.int16(-HALF_BIAS))
        lo_ref[j] = lo3.reshape(KB, nq)
        return carry

    lax.fori_loop(0, nkb, park, 0)
    t_lo = _nth_largest16(lo_ref, nkb, n_sel - n_above)
    thr = jnp.left_shift(t_hi, 16) | (t_lo + HALF_BIAS)
    need = (n_sel - n_above - _count16(lo_ref, nkb, t_lo, True)).astype(F32)
    r = lax.broadcasted_iota(I32, (KB, KB), 0)
    c = lax.broadcasted_iota(I32, (KB, KB), 1)
    tri = jnp.where(c <= r, 1.0, 0.0).astype(BF16)

    def mask_blk(j, seen):
        kk = ks_ref[j]
        eq = kk == thr
        pre = _dot(tri, jnp.where(eq, 1.0, 0.0).astype(BF16))
        tie = jnp.where(eq, jnp.where(seen + pre <= need, 0.0, NEG), NEG)
        am = jnp.where(kk > thr, 0.0, tie)
        am_ref[j] = jnp.where(kk == KEY_NEG_INF, NEG, am)
        return seen + pre[KB - 1:KB, :]

    lax.fori_loop(0, nkb, mask_blk, jnp.zeros((1, nq), F32))


def _pattn_kernel(qa_ref, qb_ref, qib_ref, wi_ref, kb_ref, vb_ref, kia_ref, kib_ref, tbl_ref, o_ref,
                  ks_scr, hi_scr, lo_scr, am_scr, vt_scr, acc_scr, p_scr, m_scr, *, n_sel):
    i = pl.program_id(1)
    n_full = i * KPQ
    nkb = n_full + KPQ
    half = D_GROUP // 2

    @pl.when(i == 0)
    def _():
        def tr(j, carry):
            ksl = pl.ds(pl.multiple_of(j * KB, KB), KB)
            vt_scr[j] = vb_ref[ksl, :].astype(F32).T.astype(BF16)
            return carry
        lax.fori_loop(0, vt_scr.shape[0], tr, 0)

    wit = wi_ref[...].T
    q01 = qib_ref[:, 0:half]
    q23 = qib_ref[:, half:D_GROUP]
    kpos0 = lax.broadcasted_iota(I32, (KB, QB), 0)
    qpos = i * QB + lax.broadcasted_iota(I32, (KB, QB), 1)

    def score_blk(diagonal):
        def body(j, carry):
            ksl = pl.ds(pl.multiple_of(j * KB, KB), KB)
            ka = kia_ref[ksl, :]
            kb2 = kib_ref[ksl, :]
            s = wit[0:1, :] * jnp.maximum(_dot_nt(ka, q01), 0.0)
            s = s + wit[1:2, :] * jnp.maximum(_dot_nt(kb2, q01), 0.0)
            s = s + wit[2:3, :] * jnp.maximum(_dot_nt(ka, q23), 0.0)
            s = s + wit[3:4, :] * jnp.maximum(_dot_nt(kb2, q23), 0.0)
            if diagonal:
                s = jnp.where(j * KB + kpos0 <= qpos, s, -jnp.inf)
            key = _sortable_key(s)
            ks_scr[j] = key
            hi_scr[j] = (key >> 16).astype(I16)
            lo_scr[j] = ((key & 0xFFFF) - HALF_BIAS).astype(I16)
            return carry
        return body

    lax.fori_loop(0, n_full, score_blk(False), 0)
    for d in range(KPQ):
        score_blk(True)(n_full + d, 0)
    _select_mask_t(ks_scr, hi_scr, lo_scr, am_scr, nkb, n_sel)

    qh = (qa_ref[:, 0:half], qb_ref[:, 0:half], qa_ref[:, half:D_GROUP], qb_ref[:, half:D_GROUP])

    def p_blk(near):
        def body(j, carry):
            ms, ls = carry
            ksl = pl.ds(pl.multiple_of(j * KB, KB), KB)
            new_m, new_l = [], []
            for h in range(N_HEADS):
                kp = kb_ref[ksl, 0:half] if h < 2 else kb_ref[ksl, half:D_GROUP]
                lg = _dot_nt(kp, qh[h]) + am_scr[j]
                if near:
                    lg = lg + tbl_ref[h, nkb - 1 - j]
                m_new = jnp.maximum(ms[h], jnp.max(_fold(lg, jnp.max), axis=0, keepdims=True))
                p = jnp.exp(lg - m_new)
                new_l.append(ls[h] * jnp.exp(ms[h] - m_new) + _fold(p, jnp.sum))
                new_m.append(m_new)
                p_scr[h, j] = p.astype(BF16)
                m_scr[h, j] = jnp.broadcast_to(m_new, (SUB, QB))
            return tuple(new_m), tuple(new_l)
        return body

    n_far = jnp.maximum(n_full - 1, 0)
    carry = (tuple(jnp.full((1, QB), -3e38, F32) for _ in range(N_HEADS)),
             tuple(jnp.zeros((SUB, QB), F32) for _ in range(N_HEADS)))
    carry = lax.fori_loop(0, n_far, p_blk(False), carry)
    ms, ls = lax.fori_loop(n_far, nkb, p_blk(True), carry)
    acc_scr[...] = jnp.zeros_like(acc_scr)

    def pv_blk(j, carry):
        for h in range(N_HEADS):
            vt = vt_scr[j, 0:half, :] if h < 2 else vt_scr[j, half:D_GROUP, :]
            acc_scr[h] += _dot(vt, p_scr[h, j]) * jnp.exp(m_scr[h, j, 0:1, :] - ms[h])
        return carry

    lax.fori_loop(0, nkb, pv_blk, 0)
    rows = []
    for h in range(N_HEADS):
        lo = (h % 2) * HEAD_DIM
        rows.append(acc_scr[h, lo:lo + HEAD_DIM, :] / jnp.sum(ls[h], axis=0, keepdims=True))
    o_ref[...] = jnp.concatenate(rows, axis=0).T


def _prompt_attention(pr, tbl, nb, lp, n_sel):
    nqb = lp // QB
    nkt = lp // KB
    m = nb * lp
    qblk = lambda c: pl.BlockSpec((QB, c), lambda b, i: (b * nqb + i, 0))
    kres = lambda c: pl.BlockSpec((lp, c), lambda b, i: (b, 0))
    return pl.pallas_call(
        functools.partial(_pattn_kernel, n_sel=n_sel),
        grid=(nb, nqb),
        in_specs=[qblk(256), qblk(256), qblk(256), qblk(128), kres(256), kres(256), kres(128), kres(128),
                  pl.BlockSpec((N_HEADS, KPQ + 1, KB, QB), lambda b, i: (0, 0, 0, 0))],
        out_specs=qblk(256),
        out_shape=jax.ShapeDtypeStruct((m, 256), F32),
        scratch_shapes=[pltpu.VMEM((nkt, KB, QB), I32), pltpu.VMEM((nkt, KB, QB), I16),
                        pltpu.VMEM((nkt, KB, QB), I16), pltpu.VMEM((nkt, KB, QB), F32),
                        pltpu.VMEM((nkt, D_GROUP, KB), BF16), pltpu.VMEM((N_HEADS, D_GROUP // 2, QB), F32),
                        pltpu.VMEM((N_HEADS, nkt, KB, QB), BF16), pltpu.VMEM((N_HEADS, nkt, SUB, QB), F32)],
        compiler_params=_params(("parallel", "arbitrary")),
        name="prompt_sparse_attention",
    )(pr["qa"], pr["qb"], pr["qib"], pr["wi"], pr["kb"], pr["vb"], pr["kia"], pr["kib"], tbl)


def _ssm_tail(y_lin, u, vec_ref, wglu_ref):
    y = y_lin + vec_ref[0:1, :] * u
    z = _gelu_tanh(y)
    return z * _sigmoid(_dot(z.astype(BF16), wglu_ref[...]))


def _conv_tail(acc, vec_ref):
    y = acc + vec_ref[1:2, :]
    mu = jnp.mean(y, axis=-1, keepdims=True)
    var = jnp.mean(jnp.square(y - mu), axis=-1, keepdims=True)
    y = (y - mu) * lax.rsqrt(var + EPS) * vec_ref[2:3, :] + vec_ref[3:4, :]
    return y * _sigmoid(y)


def _pool_pick(sums, cnts, cur):
    lane = lax.broadcasted_iota(I32, cur.shape, 1)
    grp = lane // (D_GROUP // len(POOL_WINDOWS))
    d = sums[3] / cnts[3]
    for g in (2, 1, 0):
        d = jnp.where(grp == g, sums[g] / cnts[g], d)
    return d - cur


def _pmix_kernel(us_ref, ca_ref, cg_ref, pp_ref, bbr_ref, bbi_ref, cre_ref, cim_ref, coef_ref, vec_ref,
                 wglu_ref, cw_ref, pw_ref,
                 yssm_ref, yconv_ref, ypool_ref, u_ref, hfin_ref,
                 hr_scr, hi_scr, car_scr, cext_scr, pext_scr, *, fin_chunk, fin_grp):
    i = pl.program_id(1)
    rows = BLK

    @pl.when(i == 0)
    def _():
        car_scr[...] = jnp.zeros_like(car_scr)
        cext_scr[0:32, :] = jnp.zeros((32, D_GROUP), F32)
        pext_scr[0:16, :] = jnp.zeros((16, D_GROUP), F32)

    us = us_ref[...]
    ub = us.astype(BF16)
    hr_scr[...] = _dot(ub, bbr_ref[...])
    hi_scr[...] = _dot(ub, bbi_ref[...])
    k1r, k1i, k2r, k2i, k4r, k4i, pwr, pwi = (coef_ref[n] for n in range(8))

    def grp_body(r, carry):
        cr, ci = carry
        sl = pl.ds(pl.multiple_of(r * SUB, SUB), SUB)
        xr = hr_scr[sl, :]
        xi = hi_scr[sl, :]
        for sh, (ar, ai) in ((1, (k1r, k1i)), (2, (k2r, k2i)), (4, (k4r, k4i))):
            sr = pltpu.roll(xr, sh, 0)
            si = pltpu.roll(xi, sh, 0)
            xr, xi = xr + ar * sr - ai * si, xi + ar * si + ai * sr
        hr = xr + pwr * cr - pwi * ci
        hi = xi + pwr * ci + pwi * cr
        hr_scr[sl, :] = hr
        hi_scr[sl, :] = hi
        return (jnp.broadcast_to(hr[SUB - 1:SUB, :], hr.shape), jnp.broadcast_to(hi[SUB - 1:SUB, :], hi.shape))

    cr, ci = lax.fori_loop(0, rows // SUB, grp_body, (car_scr[0], car_scr[1]))
    car_scr[0] = cr
    car_scr[1] = ci

    @pl.when(i == fin_chunk)
    def _():
        hfin_ref[0] = hr_scr[fin_grp * SUB:(fin_grp + 1) * SUB, :]
        hfin_ref[1] = hi_scr[fin_grp * SUB:(fin_grp + 1) * SUB, :]

    y_lin = _dot(hr_scr[...].astype(BF16), cre_ref[...]) - _dot(hi_scr[...].astype(BF16), cim_ref[...])
    yssm_ref[...] = _ssm_tail(y_lin, us, vec_ref, wglu_ref)

    u = ca_ref[...] * _sigmoid(cg_ref[...])
    u_ref[...] = u
    cext_scr[32:32 + rows, :] = u
    acc = jnp.zeros((rows, D_GROUP), F32)
    for k in range(CONV_WIDTH):
        acc = acc + cw_ref[k:k + 1, :] * cext_scr[2 + k:2 + k + rows, :]
    yconv_ref[...] = _conv_tail(acc, vec_ref)
    cext_scr[0:32, :] = cext_scr[rows:rows + 32, :]

    p = pp_ref[...]
    pext_scr[16:16 + rows, :] = p
    e = pext_scr[...]
    s2 = e + pltpu.roll(e, 1, 0)
    s4 = s2 + pltpu.roll(s2, 2, 0)
    s8 = s4 + pltpu.roll(s4, 4, 0)
    s16 = s8 + pltpu.roll(s8, 8, 0)
    pos1 = (i * rows + 1 + lax.broadcasted_iota(I32, (rows, D_GROUP), 0)).astype(F32)
    sums = [s[16:16 + rows, :] for s in (s2, s4, s8, s16)]
    cnts = [jnp.minimum(pos1, float(w)) for w in POOL_WINDOWS]
    d = _pool_pick(sums, cnts, p)
    ypool_ref[...] = _dot(d.astype(BF16), pw_ref[...]) * vec_ref[4:5, :]
    pext_scr[0:16, :] = pext_scr[rows:rows + 16, :]


def _prompt_mixers(pr, sp, nb, lp, t_real):
    nqb = lp // BLK
    m = nb * lp
    blk = pl.BlockSpec((BLK, D_GROUP), lambda b, i: (b * nqb + i, 0))
    full = lambda a: pl.BlockSpec(a.shape, lambda b, i: (0,) * a.ndim)
    last = t_real - 1
    weights = [sp["bbr"], sp["bbi"], sp["cre"], sp["cim"], sp["coef"], sp["vec"], sp["wglu"], sp["cw"], sp["pw"]]
    f256 = jax.ShapeDtypeStruct((m, D_GROUP), F32)
    return pl.pallas_call(
        functools.partial(_pmix_kernel, fin_chunk=last // BLK, fin_grp=(last % BLK) // SUB),
        grid=(nb, nqb),
        in_specs=[blk, blk, blk, blk] + [full(w) for w in weights],
        out_specs=[blk, blk, blk, blk, pl.BlockSpec((None, 2, SUB, SSM_LANES), lambda b, i: (b, 0, 0, 0))],
        out_shape=[f256, f256, f256, f256, jax.ShapeDtypeStruct((nb, 2, SUB, SSM_LANES), F32)],
        scratch_shapes=[pltpu.VMEM((BLK, SSM_LANES), F32), pltpu.VMEM((BLK, SSM_LANES), F32),
                        pltpu.VMEM((2, SUB, SSM_LANES), F32), pltpu.VMEM((32 + BLK, D_GROUP), F32),
                        pltpu.VMEM((16 + BLK, D_GROUP), F32)],
        compiler_params=_params(("parallel", "arbitrary")),
        name="prompt_ssm_conv_pool",
    )(pr["us"], pr["ca"], pr["cg"], pr["pp"], *weights)


def _smix_kernel(us_ref, ca_ref, cg_ref, pp_ref, h0r_ref, h0i_ref, cst_ref, pst_ref,
                 bbr_ref, bbi_ref, cre_ref, cim_ref, ab_ref, vec_ref, wglu_ref, cw_ref, pw_ref,
                 yssm_ref, yconv_ref, ypool_ref, u_ref, hr_ref, hi_ref, *, t_new):
    c = D_GROUP
    sl = lambda t: slice(t * c, (t + 1) * c)
    ar = ab_ref[0:1, :]
    ai = ab_ref[1:2, :]
    hr = h0r_ref[...]
    hi = h0i_ref[...]
    for t in range(t_new):
        us = us_ref[:, sl(t)]
        ub = us.astype(BF16)
        hr, hi = ar * hr - ai * hi + _dot(ub, bbr_ref[...]), ar * hi + ai * hr + _dot(ub, bbi_ref[...])
        y_lin = _dot(hr.astype(BF16), cre_ref[...]) - _dot(hi.astype(BF16), cim_ref[...])
        yssm_ref[:, sl(t)] = _ssm_tail(y_lin, us, vec_ref, wglu_ref)
    hr_ref[...] = hr
    hi_ref[...] = hi

    n_cst = CONV_WIDTH - 1
    ext = [cst_ref[:, sl(j)] for j in range(n_cst)]
    for t in range(t_new):
        u = ca_ref[:, sl(t)] * _sigmoid(cg_ref[:, sl(t)])
        u_ref[:, sl(t)] = u
        ext.append(u)
    for t in range(t_new):
        acc = cw_ref[0:1, :] * ext[t]
        for k in range(1, CONV_WIDTH):
            acc = acc + cw_ref[k:k + 1, :] * ext[t + k]
        yconv_ref[:, sl(t)] = _conv_tail(acc, vec_ref)

    pe = [pst_ref[:, sl(j)] for j in range(POOL_BUF)] + [pp_ref[:, sl(t)] for t in range(t_new)]
    n = len(pe)
    s2 = [None] + [pe[j] + pe[j - 1] for j in range(1, n)]
    s4 = [None] * 3 + [s2[j] + s2[j - 2] for j in range(3, n)]
    s8 = [None] * 7 + [s4[j] + s4[j - 4] for j in range(7, n)]
    s16 = [None] * 15 + [s8[j] + s8[j - 8] for j in range(15, n)]
    for t in range(t_new):
        j = POOL_BUF + t
        cnts = [float(w) for w in POOL_WINDOWS]
        d = _pool_pick([s2[j], s4[j], s8[j], s16[j]], cnts, pe[j])
        ypool_ref[:, sl(t)] = _dot(d.astype(BF16), pw_ref[...]) * vec_ref[4:5, :]


def _sample_mixers(sm, sp, h0r, h0i, cst, pst, nseq, t_new):
    wide = lambda a: a.reshape(nseq, t_new * D_GROUP)
    args = [wide(sm["us"]), wide(sm["ca"]), wide(sm["cg"]), wide(sm["pp"]), h0r, h0i, cst, pst,
            sp["bbr"], sp["bbi"], sp["cre"], sp["cim"], sp["ab"], sp["vec"], sp["wglu"], sp["cw"], sp["pw"]]
    full = lambda a: pl.BlockSpec(a.shape, lambda i: (0,) * a.ndim)
    fw = jax.ShapeDtypeStruct((nseq, t_new * D_GROUP), F32)
    fs = jax.ShapeDtypeStruct((nseq, SSM_LANES), F32)
    shapes = [fw, fw, fw, fw, fs, fs]
    return pl.pallas_call(
        functools.partial(_smix_kernel, t_new=t_new),
        grid=(1,),
        in_specs=[full(a) for a in args],
        out_specs=[full(s) for s in shapes],
        out_shape=shapes,
        compiler_params=_params(("arbitrary",)),
        name="sample_ssm_conv_pool",
    )(*args)


def _sscore_kernel(pt_ref, qib_ref, wi_ref, kin_ref, *rest, n_pages, t_new):
    page_refs = rest[:n_pages]
    s_ref = rest[n_pages]
    qif = qib_ref[...].astype(F32)
    qst = jnp.concatenate([qif[:, h * IDX_DIM:(h + 1) * IDX_DIM] for h in range(IDX_HEADS)], axis=0).astype(BF16)
    wcol = jnp.concatenate([wi_ref[:, h:h + 1] for h in range(IDX_HEADS)], axis=0)

    def score(qk):
        d = wcol * jnp.maximum(qk, 0.0)
        s = d[0:t_new]
        for h in range(1, IDX_HEADS):
            s = s + d[h * t_new:(h + 1) * t_new]
        return s

    for p in range(n_pages):
        s_ref[:, p * PAGE:(p + 1) * PAGE] = score(_dot(qst, page_refs[p][...].astype(BF16)))
    knew = jnp.concatenate([kin_ref[...].astype(F32)[:, 0:IDX_DIM], jnp.zeros((PAGE - t_new, IDX_DIM), F32)], axis=0)
    s = score(_dot_nt(qst, knew.astype(BF16)))
    r = lax.broadcasted_iota(I32, s.shape, 0)
    c = lax.broadcasted_iota(I32, s.shape, 1)
    s_ref[:, n_pages * PAGE:(n_pages + 1) * PAGE] = jnp.where(c <= r, s, -jnp.inf)


def _sample_scores(sm, cache_kidx, pt_flat, layer, nseq, t_new, n_pages):
    ncol = (n_pages + 1) * PAGE
    row = lambda c: pl.BlockSpec((t_new, c), lambda b, pt: (b, 0))
    page = lambda p: pl.BlockSpec((None, None, IDX_DIM, PAGE),
                                  lambda b, pt, p=p: (layer, pt[b * n_pages + p], 0, 0))
    grid_spec = pltpu.PrefetchScalarGridSpec(
        num_scalar_prefetch=1, grid=(nseq,),
        in_specs=[row(256), row(128), row(128)] + [page(p) for p in range(n_pages)],
        out_specs=pl.BlockSpec((None, t_new, ncol), lambda b, pt: (b, 0, 0)))
    return pl.pallas_call(
        functools.partial(_sscore_kernel, n_pages=n_pages, t_new=t_new),
        grid_spec=grid_spec,
        out_shape=jax.ShapeDtypeStruct((nseq, t_new, ncol), F32),
        compiler_params=_params(("parallel",)),
        name="sample_indexer_scores",
    )(pt_flat, sm["qib"], sm["wi"], sm["kia"], *([cache_kidx] * n_pages))


def _sselect_kernel(s_ref, am_ref, ks_scr, hi_scr, lo_scr, am_scr, *, nkb, n_sel):
    for j in range(nkb):
        key = _sortable_key(s_ref[j * KB:(j + 1) * KB, :])
        ks_scr[j] = key
        hi_scr[j] = (key >> 16).astype(I16)
        lo_scr[j] = ((key & 0xFFFF) - HALF_BIAS).astype(I16)
    _select_mask_t(ks_scr, hi_scr, lo_scr, am_scr, nkb, n_sel)
    for j in range(nkb):
        am_ref[j * KB:(j + 1) * KB, :] = am_scr[j]


def _sample_select(scores, n_sel):
    m, ncol = scores.shape
    nkb = -(-ncol // KB)
    st = jnp.concatenate([scores, jnp.full((m, nkb * KB - ncol), -jnp.inf, F32)], axis=1).T
    spec = pl.BlockSpec((nkb * KB, QB), lambda i: (0, i))
    blk = lambda dt: pltpu.VMEM((nkb, KB, QB), dt)
    amt = pl.pallas_call(
        functools.partial(_sselect_kernel, nkb=nkb, n_sel=n_sel),
        grid=(m // QB,),
        in_specs=[spec],
        out_specs=spec,
        out_shape=jax.ShapeDtypeStruct((nkb * KB, m), F32),
        scratch_shapes=[blk(I32), blk(I16), blk(I16), blk(F32)],
        compiler_params=_params(("parallel",)),
        name="sample_topn_mask",
    )(st)
    return amt.T


def _sattn_kernel(pt_ref, q_ref, am_ref, kn_ref, vn_ref, tbl_ref, *rest, n_pages, t_new):
    k_refs = rest[:n_pages]
    v_refs = rest[n_pages:2 * n_pages]
    o_ref = rest[2 * n_pages]
    q = q_ref[...].astype(F32)
    lane = lax.broadcasted_iota(I32, q.shape, 1)
    qbd = jnp.concatenate([jnp.where(lane // HEAD_DIM == h, q, 0.0) for h in range(N_HEADS)], axis=0).astype(BF16)
    zpad = jnp.zeros((PAGE - t_new, D_GROUP), F32)
    pad_page = lambda r: jnp.concatenate([r[...].astype(F32), zpad], axis=0).astype(BF16)
    lgs = []
    for p in range(n_pages + 1):
        am = am_ref[:, p * PAGE:(p + 1) * PAGE]
        am4 = jnp.concatenate([am] * N_HEADS, axis=0)
        tsel = 0 if p < n_pages - 1 else (1 if p == n_pages - 1 else 2)
        qk = _dot(qbd, k_refs[p][...].astype(BF16)) if p < n_pages else _dot_nt(qbd, pad_page(kn_ref))
        lgs.append(qk + tbl_ref[tsel] + am4)
    mx = lgs[0]
    for lg in lgs[1:]:
        mx = jnp.maximum(mx, lg)
    mx = jnp.max(mx, axis=-1, keepdims=True)
    ps = [jnp.exp(lg - mx) for lg in lgs]
    tot = ps[0]
    for pp_ in ps[1:]:
        tot = tot + pp_
    den = jnp.sum(tot, axis=-1, keepdims=True)
    acc = _dot(ps[n_pages].astype(BF16), pad_page(vn_ref))
    for p in range(n_pages):
        acc = acc + _dot_nt(ps[p].astype(BF16), v_refs[p][...].astype(BF16))
    acc = acc / den
    out = acc[0:t_new]
    for h in range(1, N_HEADS):
        out = jnp.where(lane // HEAD_DIM == h, acc[h * t_new:(h + 1) * t_new], out)
    o_ref[...] = out


def _sample_attention(sm, qs, am, tbl, cache_k, cache_v, pt_flat, layer, nseq, t_new, n_pages):
    row = lambda c: pl.BlockSpec((t_new, c), lambda b, pt: (b, 0))
    page = lambda p: pl.BlockSpec((None, None, D_GROUP, PAGE),
                                  lambda b, pt, p=p: (layer, pt[b * n_pages + p], 0, 0))
    grid_spec = pltpu.PrefetchScalarGridSpec(
        num_scalar_prefetch=1, grid=(nseq,),
        in_specs=[row(256), row(am.shape[1]), row(256), row(256),
                  pl.BlockSpec(tbl.shape, lambda b, pt: (0, 0, 0))]
        + [page(p) for p in range(n_pages)] + [page(p) for p in range(n_pages)],
        out_specs=row(256))
    return pl.pallas_call(
        functools.partial(_sattn_kernel, n_pages=n_pages, t_new=t_new),
        grid_spec=grid_spec,
        out_shape=jax.ShapeDtypeStruct((nseq * t_new, D_GROUP), F32),
        compiler_params=_params(("parallel",)),
        name="sample_sparse_attention",
    )(pt_flat, qs, am, sm["kb"], sm["vb"], tbl, *([cache_k] * n_pages), *([cache_v] * n_pages))


def _outproj_kernel(x_ref, a_ref, s_ref, c_ref, p_ref, w_ref, o_ref):
    acc = x_ref[...]
    for n, r in enumerate((a_ref, s_ref, c_ref, p_ref)):
        acc = acc + _dot(r[...].astype(BF16), w_ref[n * D_GROUP:(n + 1) * D_GROUP, :].astype(BF16))
    o_ref[...] = acc


def _outproj(x, ys, w, layer, tm):
    m, d = x.shape
    row = lambda c: pl.BlockSpec((tm, c), lambda i: (i, 0))
    return pl.pallas_call(
        _outproj_kernel,
        grid=(m // tm,),
        in_specs=[row(d)] + [row(D_GROUP)] * 4 + [pl.BlockSpec((None,) + w.shape[1:], lambda i: (layer, 0, 0))],
        out_specs=row(d),
        out_shape=jax.ShapeDtypeStruct((m, d), F32),
        compiler_params=_params(("parallel",)),
        name="mixer_out_proj",
    )(x, *ys, w)


def _rmsnorm_kernel(x_ref, g_ref, o_ref):
    x = x_ref[...]
    ms = jnp.mean(x * x, axis=-1, keepdims=True)
    o_ref[...] = x * lax.rsqrt(ms + EPS) * g_ref[...]


def _rmsnorm(x, g, tm):
    m, d = x.shape
    return pl.pallas_call(
        _rmsnorm_kernel,
        grid=(m // tm,),
        in_specs=[pl.BlockSpec((tm, d), lambda i: (i, 0)), pl.BlockSpec((1, d), lambda i: (0, 0))],
        out_specs=pl.BlockSpec((tm, d), lambda i: (i, 0)),
        out_shape=jax.ShapeDtypeStruct((m, d), F32),
        compiler_params=_params(("parallel",)),
        name="final_rmsnorm",
    )(x, g)


def _bucket_table(dist):
    n = np.maximum(dist, 0)
    max_exact = N_BUCKETS // 2
    nf = np.maximum(n, 1).astype(np.float32)
    large = max_exact + (np.log(nf / np.float32(max_exact)) / np.float32(math.log(MAX_DISTANCE / max_exact))
                         * np.float32(N_BUCKETS - max_exact)).astype(np.int32)
    large = np.minimum(large, N_BUCKETS - 1)
    return np.where(n < max_exact, n, large).astype(np.int32)


def _bias_lookup_kernel(rb_ref, idx_ref, o_ref, *, shift_bucket):
    idx = idx_ref[...]

    def body(b, accs):
        hit = idx == b
        return tuple(jnp.where(hit, rb_ref[b * N_HEADS + h], accs[h]) for h in range(N_HEADS))

    accs = lax.fori_loop(0, N_BUCKETS, body, tuple(jnp.zeros(idx.shape, F32) for _ in range(N_HEADS)))
    for h in range(N_HEADS):
        shift = 0.0 if shift_bucket is None else rb_ref[shift_bucket * N_HEADS + h]
        o_ref[h] = accs[h] - shift


def _bias_lookup(rel_bias, idx, shift_bucket):
    nt, r, c = idx.shape
    return pl.pallas_call(
        functools.partial(_bias_lookup_kernel, shift_bucket=shift_bucket),
        grid=(nt,),
        in_specs=[pl.BlockSpec(memory_space=pltpu.SMEM), pl.BlockSpec((None, r, c), lambda t: (t, 0, 0))],
        out_specs=pl.BlockSpec((N_HEADS, None, r, c), lambda t: (0, t, 0, 0)),
        out_shape=jax.ShapeDtypeStruct((N_HEADS, nt, r, c), F32),
        compiler_params=_params(("parallel",)),
        name="rel_bias_tables",
    )(rel_bias.reshape(-1).astype(F32), jnp.asarray(idx))


def _bias_tables(rel_bias, t_new):
    kk = np.arange(KB)[:, None]
    qq = np.arange(QB)[None, :]
    pidx = np.stack([_bucket_table((t - (KPQ - 1)) * KB + qq - kk) for t in range(KPQ + 1)])
    far_bucket = int(_bucket_table(np.array([2 * KB + 1]))[0])
    ptbl = _bias_lookup(rel_bias, pidx, far_bucket)
    c = np.arange(BLK)[None, :]
    t = np.arange(t_new)[:, None]
    sidx = np.stack([_bucket_table(np.full((t_new, BLK), 2 * BLK)), _bucket_table(BLK + t - c),
                     _bucket_table(t - c)])
    stbl = jnp.transpose(_bias_lookup(rel_bias, sidx, None), (1, 0, 2, 3)).reshape(3, N_HEADS * t_new, BLK)
    return ptbl, stbl


def _cmul(a, b):
    return a[0] * b[0] - a[1] * b[1], a[0] * b[1] + a[1] * b[0]


def _ssm_params(a_re, a_im, log_dt, b_re, b_im, c_re, c_im):
    nl = a_re.shape[0]
    dt = jnp.exp(log_dt)[..., None]
    decay = jnp.exp(dt * a_re)
    ab_re = decay * jnp.cos(dt * a_im)
    ab_im = decay * jnp.sin(dt * a_im)
    den = a_re * a_re + a_im * a_im
    f_re = ((ab_re - 1.0) * a_re + ab_im * a_im) / den
    f_im = (ab_im * a_re - (ab_re - 1.0) * a_im) / den
    bb_re = f_re[..., None] * b_re - f_im[..., None] * b_im
    bb_im = f_re[..., None] * b_im + f_im[..., None] * b_re
    eye = jnp.eye(SSM_GROUPS, dtype=F32)
    pack_b = lambda b: jnp.einsum("lgpc,gh->lgchp", b, eye).reshape(nl, D_GROUP, SSM_LANES).astype(BF16)
    pack_c = lambda c: jnp.einsum("lgcp,gh->lgphc", c, eye).reshape(nl, SSM_LANES, D_GROUP).astype(BF16)
    a1 = (ab_re.reshape(nl, SSM_LANES), ab_im.reshape(nl, SSM_LANES))

    def powers(a, n):
        out = [a]
        for _ in range(n - 1):
            out.append(_cmul(out[-1], a))
        return out

    bp = powers(a1, SUB)
    t = jnp.arange(SUB)[None, :, None]
    gate = lambda a, s: jnp.where(t >= s, a[:, None, :], 0.0)
    pw_re = jnp.stack([p[0] for p in bp], axis=1)
    pw_im = jnp.stack([p[1] for p in bp], axis=1)
    coef = jnp.stack([gate(bp[0][0], 1), gate(bp[0][1], 1), gate(bp[1][0], 2), gate(bp[1][1], 2),
                      gate(bp[3][0], 4), gate(bp[3][1], 4), pw_re, pw_im], axis=1)
    ab = jnp.stack([a1[0], a1[1]], axis=1)
    return pack_b(bb_re), pack_b(bb_im), pack_c(c_re), pack_c(c_im), coef, ab


def _pad_rows(a, n):
    return jnp.concatenate([a, jnp.zeros((n - a.shape[0],) + a.shape[1:], a.dtype)], axis=0)


def kernel(x_prompt, x_sample, cache_k, cache_v, cache_kidx, page_table, state_ssm_re, state_ssm_im, state_conv, state_pool, meta_tokens, rel_bias, norm_ffn1, ffn1_w_gate, ffn1_w_up, ffn1_w_down, norm_mix, w_in, w_out, ssm_a_re, ssm_a_im, ssm_log_dt, ssm_b_re, ssm_b_im, ssm_c_re, ssm_c_im, ssm_d, ssm_w_glu, conv_w, conv_b, conv_ln_g, conv_ln_b, pool_w, pool_scale, norm_ffn2, ffn2_w_gate, ffn2_w_up, ffn2_w_down, norm_final):
    nb, seq, d = x_prompt.shape
    nseq, t_new, _ = x_sample.shape
    depth = w_in.shape[0]
    n_pages = page_table.shape[1]
    n_pool = cache_k.shape[1]
    t_real = seq + N_META
    lp = -(-t_real // QB) * QB
    past = n_pages * PAGE
    n_sel_p = min(TOPK_MAX, t_real // 4)
    n_sel_s = min(TOPK_MAX, (past + t_new) // 4)
    mp, ms = nb * lp, nseq * t_new
    tm_p, tm_s = 1024, 512
    assert mp % tm_p == 0 and ms % tm_s == 0 and t_new == SUB

    meta = jnp.broadcast_to(meta_tokens[None].astype(F32), (nb, N_META, d))
    xp = jnp.concatenate([meta, x_prompt, jnp.zeros((nb, lp - t_real, d), F32)], axis=1).reshape(mp, d)
    xs = x_sample.reshape(ms, d)

    zc = lambda n: jnp.zeros((depth, d, n), F32)
    q_, k_, v_, qi_, ki_, wi_, rest = (w_in[..., 0:256], w_in[..., 256:512], w_in[..., 512:768], w_in[..., 768:1024],
                                       w_in[..., 1024:1088], w_in[..., 1088:1092], w_in[..., 1092:])
    w_in_p = jnp.concatenate([q_, k_, v_, qi_, ki_, zc(64), zc(64), ki_, wi_, zc(124), rest], axis=-1).astype(BF16)
    g_ffn1 = norm_ffn1.reshape(depth, 1, d)
    g_ffn2 = norm_ffn2.reshape(depth, 1, d)
    g_mix = norm_mix.reshape(depth, 1, d)
    bbr, bbi, cre, cim, coef, ab = _ssm_params(ssm_a_re, ssm_a_im, ssm_log_dt, ssm_b_re, ssm_b_im, ssm_c_re, ssm_c_im)
    zrow = jnp.zeros((depth, D_GROUP), F32)
    vec = jnp.stack([ssm_d, conv_b, conv_ln_g, conv_ln_b, pool_scale, zrow, zrow, zrow], axis=1)
    cw = jnp.concatenate([conv_w, jnp.zeros((depth, 1, D_GROUP), F32)], axis=1)
    ng = len(POOL_WINDOWS)
    pw = jnp.einsum("lgcd,gh->lgchd", pool_w, jnp.eye(ng, dtype=F32)).reshape(depth, D_GROUP, D_GROUP).astype(BF16)
    wglu = ssm_w_glu.astype(BF16)
    ptbl, stbl = _bias_tables(rel_bias, t_new)

    ck = jnp.transpose(cache_k, (0, 1, 3, 4, 2)).reshape(depth, n_pool, D_GROUP, PAGE)
    cv = jnp.transpose(cache_v, (0, 1, 3, 4, 2)).reshape(depth, n_pool, D_GROUP, PAGE)
    cki = jnp.transpose(cache_kidx, (0, 1, 3, 2))
    pt_flat = page_table.reshape(-1).astype(I32)
    h0r = state_ssm_re.reshape(depth, nseq, SSM_LANES)
    h0i = state_ssm_im.reshape(depth, nseq, SSM_LANES)
    cst = state_conv.reshape(depth, nseq, (CONV_WIDTH - 1) * D_GROUP)
    pst = state_pool.reshape(depth, nseq, POOL_BUF * D_GROUP)
    names = ("k", "v", "ki", "wi", "qa", "qb", "kb", "vb", "qib", "kia", "kib", "us", "ca", "cg", "pp")

    pst_out = [[] for _ in range(7)]
    sst_out = [[] for _ in range(7)]
    for l in range(depth):
        xp = _ffn(xp, g_ffn1, ffn1_w_gate, ffn1_w_up, ffn1_w_down, l, tm_p)
        xs = _ffn(xs, g_ffn1, ffn1_w_gate, ffn1_w_up, ffn1_w_down, l, tm_s)

        pr = dict(zip(names, _proj(xp, g_mix, w_in_p, l, tm_p // 2)))
        sm = dict(zip(names, _proj(xs, g_mix, w_in_p, l, tm_s)))
        sp = dict(bbr=bbr[l], bbi=bbi[l], cre=cre[l], cim=cim[l], coef=coef[l], ab=ab[l], vec=vec[l],
                  wglu=wglu[l], cw=cw[l], pw=pw[l])

        ya_p = _prompt_attention(pr, ptbl, nb, lp, n_sel_p)
        ys_p, yc_p, yp_p, u_p, hfin = _prompt_mixers(pr, sp, nb, lp, t_real)
        xp = _outproj(xp, (ya_p, ys_p, yc_p, yp_p), w_out, l, tm_p)

        scores = _sample_scores(sm, cki, pt_flat, l, nseq, t_new, n_pages)
        am = _sample_select(scores.reshape(ms, -1), n_sel_s)
        qs = sm["qa"] + sm["qb"]
        ya_s = _sample_attention(sm, qs, am, stbl, ck, cv, pt_flat, l, nseq, t_new, n_pages)
        ys_s, yc_s, yp_s, u_s, hr_s, hi_s = _sample_mixers(sm, sp, h0r[l], h0i[l], cst[l], pst[l], nseq, t_new)
        wide = lambda a: a.reshape(ms, D_GROUP)
        xs = _outproj(xs, (ya_s, wide(ys_s), wide(yc_s), wide(yp_s)), w_out, l, tm_s)

        xp = _ffn(xp, g_ffn2, ffn2_w_gate, ffn2_w_up, ffn2_w_down, l, tm_p)
        xs = _ffn(xs, g_ffn2, ffn2_w_gate, ffn2_w_up, ffn2_w_down, l, tm_s)

        seq3 = lambda a: a.reshape(nb, lp, -1)[:, :t_real]
        fin_row = (t_real - 1) % SUB
        pst_out[0].append(seq3(pr["k"]).reshape(nb, t_real, N_HEADS, HEAD_DIM))
        pst_out[1].append(seq3(pr["v"]).reshape(nb, t_real, N_HEADS, HEAD_DIM))
        pst_out[2].append(seq3(pr["ki"])[..., :IDX_DIM])
        pst_out[3].append(hfin[:, 0, fin_row].reshape(nb, SSM_GROUPS, SSM_STATE))
        pst_out[4].append(hfin[:, 1, fin_row].reshape(nb, SSM_GROUPS, SSM_STATE))
        pst_out[5].append(seq3(u_p)[:, t_real - (CONV_WIDTH - 1):])
        pst_out[6].append(seq3(pr["pp"])[:, t_real - POOL_BUF:])
        sst_out[0].append(sm["k"].reshape(nseq, t_new, N_HEADS, HEAD_DIM))
        sst_out[1].append(sm["v"].reshape(nseq, t_new, N_HEADS, HEAD_DIM))
        sst_out[2].append(sm["ki"][:, :IDX_DIM].reshape(nseq, t_new, IDX_DIM))
        sst_out[3].append(hr_s.reshape(nseq, SSM_GROUPS, SSM_STATE))
        sst_out[4].append(hi_s.reshape(nseq, SSM_GROUPS, SSM_STATE))
        sst_out[5].append(jnp.concatenate([state_conv[l][:, t_new:], u_s.reshape(nseq, t_new, D_GROUP)], axis=1))
        sst_out[6].append(jnp.concatenate([state_pool[l][:, t_new:], sm["pp"].reshape(nseq, t_new, D_GROUP)], axis=1))

    gf = norm_final[None]
    y_prompt = _rmsnorm(xp, gf, tm_p).reshape(nb, lp, d)[:, N_META:t_real]
    y_sample = _rmsnorm(xs, gf, tm_s).reshape(nseq, t_new, d)
    return (y_prompt, y_sample, *[jnp.stack(a) for a in pst_out], *[jnp.stack(a) for a in sst_out])
```

```python
import functools
import math

import numpy as np
import jax
import jax.numpy as jnp
from jax import lax
from jax.experimental import pallas as pl
from jax.experimental.pallas import tpu as pltpu

F32 = jnp.float32
BF16 = jnp.bfloat16
I32 = jnp.int32
I16 = jnp.int16

N_META = 16
N_HEADS = 4
HEAD_DIM = 64
IDX_HEADS = 4
IDX_DIM = 64
D_GROUP = 256
TOPK_MAX = 256
N_BUCKETS = 32
MAX_DISTANCE = 128
PAGE = 128
SEQ_PER_STEP = 2
SSM_GROUPS = 16
SSM_GROUP_CH = 16
SSM_STATE = 64
SSM_LANES = SSM_GROUPS * SSM_STATE
CONV_WIDTH = 31
POOL_WINDOWS = (2, 4, 8, 16)
POOL_BUF = 15
EPS = 1e-6

BLK = 128
SUB = 8
QB = 256
KB = 256
KPQ = QB // KB
HALF_ROWS = 16
HALF_BIAS = 32768
NEG = -1e30
KEY_NEG_INF = -2139095041
V7X_VMEM_BYTES = 64 * 1024 * 1024
VMEM_LIMIT = V7X_VMEM_BYTES - 8 * 1024 * 1024
TOKEN_TILE = 1024
PROJ_TILE = 512
FF_TILE = 512
PROJ_COLS = 2432


def _row_tile(m, target):
    tm = min(m, target)
    assert m % tm == 0 and tm % SUB == 0, (m, target)
    return tm


def _dot(a, b):
    return jnp.dot(a, b, preferred_element_type=F32)


def _dot_nt(a, b):
    return lax.dot_general(a, b, (((1,), (1,)), ((), ())), preferred_element_type=F32)


def _rms_bf16(x, g):
    ms = jnp.mean(x * x, axis=-1, keepdims=True)
    return (x * lax.rsqrt(ms + EPS) * g).astype(BF16)


def _sigmoid(x):
    return 1.0 / (1.0 + jnp.exp(-x))


def _gelu_tanh(x):
    c = math.sqrt(2.0 / math.pi)
    return 0.5 * x * (1.0 + jnp.tanh(c * (x + 0.044715 * (x * x * x))))


def _params(sem):
    return pltpu.CompilerParams(dimension_semantics=sem, vmem_limit_bytes=VMEM_LIMIT)


def _ffn_kernel(x_ref, g_ref, wg_ref, wu_ref, wd_ref, o_ref, h_scr, acc_scr):
    j = pl.program_id(1)

    @pl.when(j == 0)
    def _():
        h_scr[...] = _rms_bf16(x_ref[...], g_ref[...])
        acc_scr[...] = jnp.zeros_like(acc_scr)

    h = h_scr[...]
    g = _dot(h, wg_ref[...].astype(BF16))
    u = _dot(h, wu_ref[...].astype(BF16))
    a = (g * _sigmoid(g) * u).astype(BF16)
    acc_scr[...] += _dot(a, wd_ref[...].astype(BF16))

    @pl.when(j == pl.num_programs(1) - 1)
    def _():
        o_ref[...] = x_ref[...] + 0.5 * acc_scr[...]


def _ffn(x, g, wg, wu, wd, layer):
    m, d = x.shape
    tm, tf = _row_tile(m, TOKEN_TILE), FF_TILE
    dff = wg.shape[2]
    return pl.pallas_call(
        _ffn_kernel,
        grid=(m // tm, dff // tf),
        in_specs=[
            pl.BlockSpec((tm, d), lambda i, j: (i, 0)),
            pl.BlockSpec((None, 1, d), lambda i, j: (layer, 0, 0)),
            pl.BlockSpec((None, d, tf), lambda i, j: (layer, 0, j)),
            pl.BlockSpec((None, d, tf), lambda i, j: (layer, 0, j)),
            pl.BlockSpec((None, tf, d), lambda i, j: (layer, j, 0)),
        ],
        out_specs=pl.BlockSpec((tm, d), lambda i, j: (i, 0)),
        out_shape=jax.ShapeDtypeStruct((m, d), F32),
        scratch_shapes=[pltpu.VMEM((tm, d), BF16), pltpu.VMEM((tm, d), F32)],
        compiler_params=_params(("parallel", "arbitrary")),
        name="half_ffn",
    )(x, g, wg, wu, wd)


def _proj_kernel(x_ref, g_ref, w_ref, k_ref, v_ref, ki_ref, wi_ref, qa_ref, qb_ref, kb_ref, vb_ref,
                 qib_ref, kia_ref, kib_ref, us_ref, ca_ref, cg_ref, pp_ref):
    h = _rms_bf16(x_ref[...], g_ref[...])
    z = _dot(h, w_ref[...])
    q = z[:, 0:256] * (HEAD_DIM ** -0.5)
    lane = lax.broadcasted_iota(I32, q.shape, 1)
    even = ((lane // HEAD_DIM) % 2) == 0
    qa_ref[...] = jnp.where(even, q, 0.0).astype(BF16)
    qb_ref[...] = jnp.where(even, 0.0, q).astype(BF16)
    k = z[:, 256:512]
    v = z[:, 512:768]
    k_ref[...] = k
    v_ref[...] = v
    kb_ref[...] = k.astype(BF16)
    vb_ref[...] = v.astype(BF16)
    qib_ref[...] = z[:, 768:1024].astype(BF16)
    kia = z[:, 1024:1152]
    ki_ref[...] = kia
    kia_ref[...] = kia.astype(BF16)
    kib_ref[...] = z[:, 1152:1280].astype(BF16)
    wi_ref[...] = z[:, 1280:1408]
    us_ref[...] = z[:, 1408:1664]
    ca_ref[...] = z[:, 1664:1920]
    cg_ref[...] = z[:, 1920:2176]
    pp_ref[...] = z[:, 2176:2432]


def _proj(x, g, w, layer):
    m, d = x.shape
    tm = _row_tile(m, PROJ_TILE)
    row = lambda i: (i, 0)
    f256 = jax.ShapeDtypeStruct((m, 256), F32)
    f128 = jax.ShapeDtypeStruct((m, 128), F32)
    b256 = jax.ShapeDtypeStruct((m, 256), BF16)
    b128 = jax.ShapeDtypeStruct((m, 128), BF16)
    shapes = [f256, f256, f128, f128, b256, b256, b256, b256, b256, b128, b128, f256, f256, f256, f256]
    return pl.pallas_call(
        _proj_kernel,
        grid=(m // tm,),
        in_specs=[
            pl.BlockSpec((tm, d), row),
            pl.BlockSpec((None, 1, d), lambda i: (layer, 0, 0)),
            pl.BlockSpec((None, d, PROJ_COLS), lambda i: (layer, 0, 0)),
        ],
        out_specs=[pl.BlockSpec((tm, s.shape[1]), row) for s in shapes],
        out_shape=shapes,
        compiler_params=_params(("parallel",)),
        name="mixer_in_proj",
    )(x, g, w)


def _sortable_key(s):
    b = lax.bitcast_convert_type(s, I32)
    k = b ^ ((b >> 31) & 0x7FFFFFFF)
    return jnp.where(k == -1, 0, k)


def _fold(x, op):
    x3 = x.reshape(x.shape[0] // SUB, SUB, x.shape[1])
    return op(x3, axis=0)


def _tree_sum(parts):
    while len(parts) > 1:
        parts = [parts[n] + parts[n + 1] for n in range(0, len(parts) - 1, 2)] + parts[len(parts) & ~1:]
    return parts[0]


def _count16(ref, nkb, ref_val, strict):
    nq = ref.shape[2]
    cs = jnp.broadcast_to(ref_val, (HALF_ROWS, nq)).astype(I16)[None]
    one, zero = jnp.int16(1), jnp.int16(0)

    def blk(j, pc):
        x3 = ref[j].reshape(KB // HALF_ROWS, HALF_ROWS, nq)
        m3 = jnp.where(x3 > cs if strict else x3 >= cs, one, zero)
        return pc + _tree_sum([m3[g] for g in range(KB // HALF_ROWS)])

    pc = lax.fori_loop(0, nkb, blk, jnp.zeros((HALF_ROWS, nq), I16))
    return jnp.sum(pc.astype(I32), axis=0, keepdims=True)


def _nth_largest16(ref, nkb, n_need):
    nq = ref.shape[2]

    def bit_body(bi, t):
        c = t | jnp.left_shift(jnp.int32(1), 15 - bi)
        return jnp.where(_count16(ref, nkb, c - HALF_BIAS, False) >= n_need, c, t)

    return lax.fori_loop(0, 16, bit_body, jnp.zeros((1, nq), I32)) - HALF_BIAS


def _select_mask_t(ks_ref, hi_ref, lo_ref, am_ref, nkb, n_sel):
    nq = ks_ref.shape[2]
    t_hi = _nth_largest16(hi_ref, nkb, n_sel)
    n_above = _count16(hi_ref, nkb, t_hi, True)
    t_hi16 = jnp.broadcast_to(t_hi, (HALF_ROWS, nq)).astype(I16)[None]

    def park(j, carry):
        shape3 = (KB // HALF_ROWS, HALF_ROWS, nq)
        lo3 = jnp.where(hi_ref[j].reshape(shape3) == t_hi16, lo_ref[j].reshape(shape3), jnp.int16(-HALF_BIAS))
        lo_ref[j] = lo3.reshape(KB, nq)
        return carry

    lax.fori_loop(0, nkb, park, 0)
    t_lo = _nth_largest16(lo_ref, nkb, n_sel - n_above)
    thr = jnp.left_shift(t_hi, 16) | (t_lo + HALF_BIAS)
    need = (n_sel - n_above - _count16(lo_ref, nkb, t_lo, True)).astype(F32)
    r = lax.broadcasted_iota(I32, (KB, KB), 0)
    c = lax.broadcasted_iota(I32, (KB, KB), 1)
    tri = jnp.where(c <= r, 1.0, 0.0).astype(BF16)

    def mask_blk(j, seen):
        kk = ks_ref[j]
        eq = kk == thr
        pre = _dot(tri, jnp.where(eq, 1.0, 0.0).astype(BF16))
        tie = jnp.where(eq, jnp.where(seen + pre <= need, 0.0, NEG), NEG)
        am = jnp.where(kk > thr, 0.0, tie)
        am_ref[j] = jnp.where(kk == KEY_NEG_INF, NEG, am)
        return seen + pre[KB - 1:KB, :]

    lax.fori_loop(0, nkb, mask_blk, jnp.zeros((1, nq), F32))


def _pattn_kernel(qa_ref, qb_ref, qib_ref, wi_ref, kb_ref, vb_ref, kia_ref, kib_ref, tbl_ref, o_ref,
                  ks_scr, hi_scr, lo_scr, am_scr, vt_scr, acc_scr, p_scr, m_scr, *, n_sel):
    i = pl.program_id(1)
    n_full = i * KPQ
    nkb = n_full + KPQ
    half = D_GROUP // 2

    @pl.when(i == 0)
    def _():
        def tr(j, carry):
            ksl = pl.ds(pl.multiple_of(j * KB, KB), KB)
            vt_scr[j] = vb_ref[ksl, :].astype(F32).T.astype(BF16)
            return carry
        lax.fori_loop(0, vt_scr.shape[0], tr, 0)

    wit = wi_ref[...].T
    q01 = qib_ref[:, 0:half]
    q23 = qib_ref[:, half:D_GROUP]
    kpos0 = lax.broadcasted_iota(I32, (KB, QB), 0)
    qpos = i * QB + lax.broadcasted_iota(I32, (KB, QB), 1)

    def score_blk(diagonal):
        def body(j, carry):
            ksl = pl.ds(pl.multiple_of(j * KB, KB), KB)
            ka = kia_ref[ksl, :]
            kb2 = kib_ref[ksl, :]
            s = wit[0:1, :] * jnp.maximum(_dot_nt(ka, q01), 0.0)
            s = s + wit[1:2, :] * jnp.maximum(_dot_nt(kb2, q01), 0.0)
            s = s + wit[2:3, :] * jnp.maximum(_dot_nt(ka, q23), 0.0)
            s = s + wit[3:4, :] * jnp.maximum(_dot_nt(kb2, q23), 0.0)
            if diagonal:
                s = jnp.where(j * KB + kpos0 <= qpos, s, -jnp.inf)
            key = _sortable_key(s)
            ks_scr[j] = key
            hi_scr[j] = (key >> 16).astype(I16)
            lo_scr[j] = ((key & 0xFFFF) - HALF_BIAS).astype(I16)
            return carry
        return body

    lax.fori_loop(0, n_full, score_blk(False), 0)
    for d in range(KPQ):
        score_blk(True)(n_full + d, 0)
    _select_mask_t(ks_scr, hi_scr, lo_scr, am_scr, nkb, n_sel)

    qh = (qa_ref[:, 0:half], qb_ref[:, 0:half], qa_ref[:, half:D_GROUP], qb_ref[:, half:D_GROUP])

    def p_blk(near):
        def body(j, carry):
            ms, ls = carry
            ksl = pl.ds(pl.multiple_of(j * KB, KB), KB)
            new_m, new_l = [], []
            for h in range(N_HEADS):
                kp = kb_ref[ksl, 0:half] if h < 2 else kb_ref[ksl, half:D_GROUP]
                lg = _dot_nt(kp, qh[h]) + am_scr[j]
                if near:
                    lg = lg + tbl_ref[h, nkb - 1 - j]
                m_new = jnp.maximum(ms[h], jnp.max(_fold(lg, jnp.max), axis=0, keepdims=True))
                p = jnp.exp(lg - m_new)
                new_l.append(ls[h] * jnp.exp(ms[h] - m_new) + _fold(p, jnp.sum))
                new_m.append(m_new)
                p_scr[h, j] = p.astype(BF16)
                m_scr[h, j] = jnp.broadcast_to(m_new, (SUB, QB))
            return tuple(new_m), tuple(new_l)
        return body

    n_far = jnp.maximum(n_full - 1, 0)
    carry = (tuple(jnp.full((1, QB), -3e38, F32) for _ in range(N_HEADS)),
             tuple(jnp.zeros((SUB, QB), F32) for _ in range(N_HEADS)))
    carry = lax.fori_loop(0, n_far, p_blk(False), carry)
    ms, ls = lax.fori_loop(n_far, nkb, p_blk(True), carry)
    acc_scr[...] = jnp.zeros_like(acc_scr)

    def pv_blk(j, carry):
        for h in range(N_HEADS):
            vt = vt_scr[j, 0:half, :] if h < 2 else vt_scr[j, half:D_GROUP, :]
            acc_scr[h] += _dot(vt, p_scr[h, j]) * jnp.exp(m_scr[h, j, 0:1, :] - ms[h])
        return carry

    lax.fori_loop(0, nkb, pv_blk, 0)
    rows = []
    for h in range(N_HEADS):
        lo = (h % 2) * HEAD_DIM
        rows.append(acc_scr[h, lo:lo + HEAD_DIM, :] / jnp.sum(ls[h], axis=0, keepdims=True))
    o_ref[...] = jnp.concatenate(rows, axis=0).T


def _prompt_attention(pr, tbl, nb, lp, n_sel):
    nqb = lp // QB
    nkt = lp // KB
    m = nb * lp
    qblk = lambda c: pl.BlockSpec((QB, c), lambda b, i: (b * nqb + i, 0))
    kres = lambda c: pl.BlockSpec((lp, c), lambda b, i: (b, 0))
    return pl.pallas_call(
        functools.partial(_pattn_kernel, n_sel=n_sel),
        grid=(nb, nqb),
        in_specs=[qblk(256), qblk(256), qblk(256), qblk(128), kres(256), kres(256), kres(128), kres(128),
                  pl.BlockSpec((N_HEADS, KPQ + 1, KB, QB), lambda b, i: (0, 0, 0, 0))],
        out_specs=qblk(256),
        out_shape=jax.ShapeDtypeStruct((m, 256), F32),
        scratch_shapes=[pltpu.VMEM((nkt, KB, QB), I32), pltpu.VMEM((nkt, KB, QB), I16),
                        pltpu.VMEM((nkt, KB, QB), I16), pltpu.VMEM((nkt, KB, QB), F32),
                        pltpu.VMEM((nkt, D_GROUP, KB), BF16), pltpu.VMEM((N_HEADS, D_GROUP // 2, QB), F32),
                        pltpu.VMEM((N_HEADS, nkt, KB, QB), BF16), pltpu.VMEM((N_HEADS, nkt, SUB, QB), F32)],
        compiler_params=_params(("parallel", "arbitrary")),
        name="prompt_sparse_attention",
    )(pr["qa"], pr["qb"], pr["qib"], pr["wi"], pr["kb"], pr["vb"], pr["kia"], pr["kib"], tbl)


def _ssm_tail(y_lin, u, vec_ref, wglu_ref):
    y = y_lin + vec_ref[0:1, :] * u
    z = _gelu_tanh(y)
    return z * _sigmoid(_dot(z.astype(BF16), wglu_ref[...]))


def _conv_tail(acc, vec_ref):
    y = acc + vec_ref[1:2, :]
    mu = jnp.mean(y, axis=-1, keepdims=True)
    var = jnp.mean(jnp.square(y - mu), axis=-1, keepdims=True)
    y = (y - mu) * lax.rsqrt(var + EPS) * vec_ref[2:3, :] + vec_ref[3:4, :]
    return y * _sigmoid(y)


def _pool_pick(sums, cnts, cur):
    lane = lax.broadcasted_iota(I32, cur.shape, 1)
    grp = lane // (D_GROUP // len(POOL_WINDOWS))
    d = sums[3] / cnts[3]
    for g in (2, 1, 0):
        d = jnp.where(grp == g, sums[g] / cnts[g], d)
    return d - cur


def _pmix_kernel(us_ref, ca_ref, cg_ref, pp_ref, bbr_ref, bbi_ref, cre_ref, cim_ref, coef_ref, vec_ref,
                 wglu_ref, cw_ref, pw_ref,
                 yssm_ref, yconv_ref, ypool_ref, u_ref, hfin_ref,
                 hr_scr, hi_scr, car_scr, cext_scr, pext_scr, *, fin_chunk, fin_grp):
    i = pl.program_id(1)
    rows = BLK

    @pl.when(i == 0)
    def _():
        car_scr[...] = jnp.zeros_like(car_scr)
        cext_scr[0:32, :] = jnp.zeros((32, D_GROUP), F32)
        pext_scr[0:16, :] = jnp.zeros((16, D_GROUP), F32)

    us = us_ref[...]
    ub = us.astype(BF16)
    hr_scr[...] = _dot(ub, bbr_ref[...])
    hi_scr[...] = _dot(ub, bbi_ref[...])
    k1r, k1i, k2r, k2i, k4r, k4i, pwr, pwi = (coef_ref[n] for n in range(8))

    def grp_body(r, carry):
        cr, ci = carry
        sl = pl.ds(pl.multiple_of(r * SUB, SUB), SUB)
        xr = hr_scr[sl, :]
        xi = hi_scr[sl, :]
        for sh, (ar, ai) in ((1, (k1r, k1i)), (2, (k2r, k2i)), (4, (k4r, k4i))):
            sr = pltpu.roll(xr, sh, 0)
            si = pltpu.roll(xi, sh, 0)
            xr, xi = xr + ar * sr - ai * si, xi + ar * si + ai * sr
        hr = xr + pwr * cr - pwi * ci
        hi = xi + pwr * ci + pwi * cr
        hr_scr[sl, :] = hr
        hi_scr[sl, :] = hi
        return (jnp.broadcast_to(hr[SUB - 1:SUB, :], hr.shape), jnp.broadcast_to(hi[SUB - 1:SUB, :], hi.shape))

    cr, ci = lax.fori_loop(0, rows // SUB, grp_body, (car_scr[0], car_scr[1]))
    car_scr[0] = cr
    car_scr[1] = ci

    @pl.when(i == fin_chunk)
    def _():
        hfin_ref[0] = hr_scr[fin_grp * SUB:(fin_grp + 1) * SUB, :]
        hfin_ref[1] = hi_scr[fin_grp * SUB:(fin_grp + 1) * SUB, :]

    y_lin = _dot(hr_scr[...].astype(BF16), cre_ref[...]) - _dot(hi_scr[...].astype(BF16), cim_ref[...])
    yssm_ref[...] = _ssm_tail(y_lin, us, vec_ref, wglu_ref)

    u = ca_ref[...] * _sigmoid(cg_ref[...])
    u_ref[...] = u
    cext_scr[32:32 + rows, :] = u
    acc = jnp.zeros((rows, D_GROUP), F32)
    for k in range(CONV_WIDTH):
        acc = acc + cw_ref[k:k + 1, :] * cext_scr[2 + k:2 + k + rows, :]
    yconv_ref[...] = _conv_tail(acc, vec_ref)
    cext_scr[0:32, :] = cext_scr[rows:rows + 32, :]

    p = pp_ref[...]
    pext_scr[16:16 + rows, :] = p
    e = pext_scr[...]
    s2 = e + pltpu.roll(e, 1, 0)
    s4 = s2 + pltpu.roll(s2, 2, 0)
    s8 = s4 + pltpu.roll(s4, 4, 0)
    s16 = s8 + pltpu.roll(s8, 8, 0)
    pos1 = (i * rows + 1 + lax.broadcasted_iota(I32, (rows, D_GROUP), 0)).astype(F32)
    sums = [s[16:16 + rows, :] for s in (s2, s4, s8, s16)]
    cnts = [jnp.minimum(pos1, float(w)) for w in POOL_WINDOWS]
    d = _pool_pick(sums, cnts, p)
    ypool_ref[...] = _dot(d.astype(BF16), pw_ref[...]) * vec_ref[4:5, :]
    pext_scr[0:16, :] = pext_scr[rows:rows + 16, :]


def _prompt_mixers(pr, sp, nb, lp, t_real):
    nqb = lp // BLK
    m = nb * lp
    blk = pl.BlockSpec((BLK, D_GROUP), lambda b, i: (b * nqb + i, 0))
    full = lambda a: pl.BlockSpec(a.shape, lambda b, i: (0,) * a.ndim)
    last = t_real - 1
    weights = [sp["bbr"], sp["bbi"], sp["cre"], sp["cim"], sp["coef"], sp["vec"], sp["wglu"], sp["cw"], sp["pw"]]
    f256 = jax.ShapeDtypeStruct((m, D_GROUP), F32)
    return pl.pallas_call(
        functools.partial(_pmix_kernel, fin_chunk=last // BLK, fin_grp=(last % BLK) // SUB),
        grid=(nb, nqb),
        in_specs=[blk, blk, blk, blk] + [full(w) for w in weights],
        out_specs=[blk, blk, blk, blk, pl.BlockSpec((None, 2, SUB, SSM_LANES), lambda b, i: (b, 0, 0, 0))],
        out_shape=[f256, f256, f256, f256, jax.ShapeDtypeStruct((nb, 2, SUB, SSM_LANES), F32)],
        scratch_shapes=[pltpu.VMEM((BLK, SSM_LANES), F32), pltpu.VMEM((BLK, SSM_LANES), F32),
                        pltpu.VMEM((2, SUB, SSM_LANES), F32), pltpu.VMEM((32 + BLK, D_GROUP), F32),
                        pltpu.VMEM((16 + BLK, D_GROUP), F32)],
        compiler_params=_params(("parallel", "arbitrary")),
        name="prompt_ssm_conv_pool",
    )(pr["us"], pr["ca"], pr["cg"], pr["pp"], *weights)


def _smix_kernel(us_ref, ca_ref, cg_ref, pp_ref, h0r_ref, h0i_ref, cst_ref, pst_ref,
                 bbr_ref, bbi_ref, cre_ref, cim_ref, ab_ref, vec_ref, wglu_ref, cw_ref, pw_ref,
                 yssm_ref, yconv_ref, ypool_ref, u_ref, hr_ref, hi_ref, *, t_new):
    c = D_GROUP
    sl = lambda t: slice(t * c, (t + 1) * c)
    ar = ab_ref[0:1, :]
    ai = ab_ref[1:2, :]
    hr = h0r_ref[...]
    hi = h0i_ref[...]
    for t in range(t_new):
        us = us_ref[:, sl(t)]
        ub = us.astype(BF16)
        hr, hi = ar * hr - ai * hi + _dot(ub, bbr_ref[...]), ar * hi + ai * hr + _dot(ub, bbi_ref[...])
        y_lin = _dot(hr.astype(BF16), cre_ref[...]) - _dot(hi.astype(BF16), cim_ref[...])
        yssm_ref[:, sl(t)] = _ssm_tail(y_lin, us, vec_ref, wglu_ref)
    hr_ref[...] = hr
    hi_ref[...] = hi

    n_cst = CONV_WIDTH - 1
    ext = [cst_ref[:, sl(j)] for j in range(n_cst)]
    for t in range(t_new):
        u = ca_ref[:, sl(t)] * _sigmoid(cg_ref[:, sl(t)])
        u_ref[:, sl(t)] = u
        ext.append(u)
    for t in range(t_new):
        acc = cw_ref[0:1, :] * ext[t]
        for k in range(1, CONV_WIDTH):
            acc = acc + cw_ref[k:k + 1, :] * ext[t + k]
        yconv_ref[:, sl(t)] = _conv_tail(acc, vec_ref)

    pe = [pst_ref[:, sl(j)] for j in range(POOL_BUF)] + [pp_ref[:, sl(t)] for t in range(t_new)]
    n = len(pe)
    s2 = [None] + [pe[j] + pe[j - 1] for j in range(1, n)]
    s4 = [None] * 3 + [s2[j] + s2[j - 2] for j in range(3, n)]
    s8 = [None] * 7 + [s4[j] + s4[j - 4] for j in range(7, n)]
    s16 = [None] * 15 + [s8[j] + s8[j - 8] for j in range(15, n)]
    for t in range(t_new):
        j = POOL_BUF + t
        cnts = [float(w) for w in POOL_WINDOWS]
        d = _pool_pick([s2[j], s4[j], s8[j], s16[j]], cnts, pe[j])
        ypool_ref[:, sl(t)] = _dot(d.astype(BF16), pw_ref[...]) * vec_ref[4:5, :]


def _sample_mixers(sm, sp, h0r, h0i, cst, pst, nseq, t_new):
    wide = lambda a: a.reshape(nseq, t_new * D_GROUP)
    args = [wide(sm["us"]), wide(sm["ca"]), wide(sm["cg"]), wide(sm["pp"]), h0r, h0i, cst, pst,
            sp["bbr"], sp["bbi"], sp["cre"], sp["cim"], sp["ab"], sp["vec"], sp["wglu"], sp["cw"], sp["pw"]]
    full = lambda a: pl.BlockSpec(a.shape, lambda i: (0,) * a.ndim)
    fw = jax.ShapeDtypeStruct((nseq, t_new * D_GROUP), F32)
    fs = jax.ShapeDtypeStruct((nseq, SSM_LANES), F32)
    shapes = [fw, fw, fw, fw, fs, fs]
    return pl.pallas_call(
        functools.partial(_smix_kernel, t_new=t_new),
        grid=(1,),
        in_specs=[full(a) for a in args],
        out_specs=[full(s) for s in shapes],
        out_shape=shapes,
        compiler_params=_params(("arbitrary",)),
        name="sample_ssm_conv_pool",
    )(*args)


def _sscore_kernel(pt_ref, qib_ref, wi_ref, kin_ref, *rest, n_pages, t_new):
    page_refs = rest[:SEQ_PER_STEP * n_pages]
    s_ref = rest[SEQ_PER_STEP * n_pages]
    for g in range(SEQ_PER_STEP):
        rows = slice(g * t_new, (g + 1) * t_new)
        qif = qib_ref[rows, :].astype(F32)
        qst = jnp.concatenate([qif[:, h * IDX_DIM:(h + 1) * IDX_DIM] for h in range(IDX_HEADS)], axis=0).astype(BF16)
        wcol = jnp.concatenate([wi_ref[rows, h:h + 1] for h in range(IDX_HEADS)], axis=0)

        def score(qk):
            d = wcol * jnp.maximum(qk, 0.0)
            s = d[0:t_new]
            for h in range(1, IDX_HEADS):
                s = s + d[h * t_new:(h + 1) * t_new]
            return s

        for p in range(n_pages):
            page = page_refs[g * n_pages + p][...].astype(BF16)
            s_ref[g, :, p * PAGE:(p + 1) * PAGE] = score(_dot(qst, page))
        knew = jnp.concatenate([kin_ref[rows, :].astype(F32)[:, 0:IDX_DIM],
                                jnp.zeros((PAGE - t_new, IDX_DIM), F32)], axis=0)
        s = score(_dot_nt(qst, knew.astype(BF16)))
        r = lax.broadcasted_iota(I32, s.shape, 0)
        c = lax.broadcasted_iota(I32, s.shape, 1)
        s_ref[g, :, n_pages * PAGE:(n_pages + 1) * PAGE] = jnp.where(c <= r, s, -jnp.inf)


def _page_specs(shape, layer, n_pages):
    def spec(g, p):
        return pl.BlockSpec((None, None) + shape,
                            lambda b, pt: (layer, pt[(b * SEQ_PER_STEP + g) * n_pages + p], 0, 0))
    return [spec(g, p) for g in range(SEQ_PER_STEP) for p in range(n_pages)]


def _sample_scores(sm, cache_kidx, pt_flat, layer, nseq, t_new, n_pages):
    ncol = (n_pages + 1) * PAGE
    row = lambda c: pl.BlockSpec((SEQ_PER_STEP * t_new, c), lambda b, pt: (b, 0))
    pages = _page_specs((IDX_DIM, PAGE), layer, n_pages)
    grid_spec = pltpu.PrefetchScalarGridSpec(
        num_scalar_prefetch=1, grid=(nseq // SEQ_PER_STEP,),
        in_specs=[row(256), row(128), row(128)] + pages,
        out_specs=pl.BlockSpec((SEQ_PER_STEP, t_new, ncol), lambda b, pt: (b, 0, 0)))
    return pl.pallas_call(
        functools.partial(_sscore_kernel, n_pages=n_pages, t_new=t_new),
        grid_spec=grid_spec,
        out_shape=jax.ShapeDtypeStruct((nseq, t_new, ncol), F32),
        compiler_params=_params(("parallel",)),
        name="sample_indexer_scores",
    )(pt_flat, sm["qib"], sm["wi"], sm["kia"], *([cache_kidx] * len(pages)))


def _sselect_kernel(s_ref, am_ref, ks_scr, hi_scr, lo_scr, am_scr, *, nkb, n_sel):
    for j in range(nkb):
        key = _sortable_key(s_ref[j * KB:(j + 1) * KB, :])
        ks_scr[j] = key
        hi_scr[j] = (key >> 16).astype(I16)
        lo_scr[j] = ((key & 0xFFFF) - HALF_BIAS).astype(I16)
    _select_mask_t(ks_scr, hi_scr, lo_scr, am_scr, nkb, n_sel)
    for j in range(nkb):
        am_ref[j * KB:(j + 1) * KB, :] = am_scr[j]


def _sample_select(scores, n_sel):
    m, ncol = scores.shape
    nkb = -(-ncol // KB)
    st = jnp.concatenate([scores, jnp.full((m, nkb * KB - ncol), -jnp.inf, F32)], axis=1).T
    spec = pl.BlockSpec((nkb * KB, QB), lambda i: (0, i))
    blk = lambda dt: pltpu.VMEM((nkb, KB, QB), dt)
    amt = pl.pallas_call(
        functools.partial(_sselect_kernel, nkb=nkb, n_sel=n_sel),
        grid=(m // QB,),
        in_specs=[spec],
        out_specs=spec,
        out_shape=jax.ShapeDtypeStruct((nkb * KB, m), F32),
        scratch_shapes=[blk(I32), blk(I16), blk(I16), blk(F32)],
        compiler_params=_params(("parallel",)),
        name="sample_topn_mask",
    )(st)
    return amt.T


def _sattn_kernel(pt_ref, q_ref, am_ref, kn_ref, vn_ref, tbl_ref, *rest, n_pages, t_new):
    n_refs = SEQ_PER_STEP * n_pages
    o_ref = rest[2 * n_refs]
    zpad = jnp.zeros((PAGE - t_new, D_GROUP), F32)
    for g in range(SEQ_PER_STEP):
        rows = slice(g * t_new, (g + 1) * t_new)
        k_refs = rest[g * n_pages:(g + 1) * n_pages]
        v_refs = rest[n_refs + g * n_pages:n_refs + (g + 1) * n_pages]
        q = q_ref[rows, :].astype(F32)
        lane = lax.broadcasted_iota(I32, q.shape, 1)
        qbd = jnp.concatenate([jnp.where(lane // HEAD_DIM == h, q, 0.0) for h in range(N_HEADS)], axis=0).astype(BF16)
        pad_page = lambda r: jnp.concatenate([r[rows, :].astype(F32), zpad], axis=0).astype(BF16)
        lgs = []
        for p in range(n_pages + 1):
            am = am_ref[rows, p * PAGE:(p + 1) * PAGE]
            am4 = jnp.concatenate([am] * N_HEADS, axis=0)
            tsel = 0 if p < n_pages - 1 else (1 if p == n_pages - 1 else 2)
            qk = _dot(qbd, k_refs[p][...].astype(BF16)) if p < n_pages else _dot_nt(qbd, pad_page(kn_ref))
            lgs.append(qk + tbl_ref[tsel] + am4)
        mx = lgs[0]
        for lg in lgs[1:]:
            mx = jnp.maximum(mx, lg)
        mx = jnp.max(mx, axis=-1, keepdims=True)
        ps = [jnp.exp(lg - mx) for lg in lgs]
        tot = ps[0]
        for pp_ in ps[1:]:
            tot = tot + pp_
        den = jnp.sum(tot, axis=-1, keepdims=True)
        acc = _dot(ps[n_pages].astype(BF16), pad_page(vn_ref))
        for p in range(n_pages):
            acc = acc + _dot_nt(ps[p].astype(BF16), v_refs[p][...].astype(BF16))
        acc = acc / den
        out = acc[0:t_new]
        for h in range(1, N_HEADS):
            out = jnp.where(lane // HEAD_DIM == h, acc[h * t_new:(h + 1) * t_new], out)
        o_ref[rows, :] = out


def _sample_attention(sm, qs, am, tbl, cache_k, cache_v, pt_flat, layer, nseq, t_new, n_pages):
    row = lambda c: pl.BlockSpec((SEQ_PER_STEP * t_new, c), lambda b, pt: (b, 0))
    pages = _page_specs((D_GROUP, PAGE), layer, n_pages)
    grid_spec = pltpu.PrefetchScalarGridSpec(
        num_scalar_prefetch=1, grid=(nseq // SEQ_PER_STEP,),
        in_specs=[row(256), row(am.shape[1]), row(256), row(256),
                  pl.BlockSpec(tbl.shape, lambda b, pt: (0, 0, 0))]
        + pages + _page_specs((D_GROUP, PAGE), layer, n_pages),
        out_specs=row(256))
    return pl.pallas_call(
        functools.partial(_sattn_kernel, n_pages=n_pages, t_new=t_new),
        grid_spec=grid_spec,
        out_shape=jax.ShapeDtypeStruct((nseq * t_new, D_GROUP), F32),
        compiler_params=_params(("parallel",)),
        name="sample_sparse_attention",
    )(pt_flat, qs, am, sm["kb"], sm["vb"], tbl, *([cache_k] * len(pages)), *([cache_v] * len(pages)))


def _outproj_kernel(x_ref, a_ref, s_ref, c_ref, p_ref, w_ref, o_ref):
    acc = x_ref[...]
    for n, r in enumerate((a_ref, s_ref, c_ref, p_ref)):
        acc = acc + _dot(r[...].astype(BF16), w_ref[n * D_GROUP:(n + 1) * D_GROUP, :].astype(BF16))
    o_ref[...] = acc


def _outproj(x, ys, w, layer):
    m, d = x.shape
    tm = _row_tile(m, TOKEN_TILE)
    row = lambda c: pl.BlockSpec((tm, c), lambda i: (i, 0))
    return pl.pallas_call(
        _outproj_kernel,
        grid=(m // tm,),
        in_specs=[row(d)] + [row(D_GROUP)] * 4 + [pl.BlockSpec((None,) + w.shape[1:], lambda i: (layer, 0, 0))],
        out_specs=row(d),
        out_shape=jax.ShapeDtypeStruct((m, d), F32),
        compiler_params=_params(("parallel",)),
        name="mixer_out_proj",
    )(x, *ys, w)


def _rmsnorm_kernel(x_ref, g_ref, o_ref):
    x = x_ref[...]
    ms = jnp.mean(x * x, axis=-1, keepdims=True)
    o_ref[...] = x * lax.rsqrt(ms + EPS) * g_ref[...]


def _rmsnorm(x, g):
    m, d = x.shape
    tm = _row_tile(m, TOKEN_TILE)
    return pl.pallas_call(
        _rmsnorm_kernel,
        grid=(m // tm,),
        in_specs=[pl.BlockSpec((tm, d), lambda i: (i, 0)), pl.BlockSpec((1, d), lambda i: (0, 0))],
        out_specs=pl.BlockSpec((tm, d), lambda i: (i, 0)),
        out_shape=jax.ShapeDtypeStruct((m, d), F32),
        compiler_params=_params(("parallel",)),
        name="final_rmsnorm",
    )(x, g)


def _bucket_table(dist):
    n = np.maximum(dist, 0)
    max_exact = N_BUCKETS // 2
    nf = np.maximum(n, 1).astype(np.float32)
    large = max_exact + (np.log(nf / np.float32(max_exact)) / np.float32(math.log(MAX_DISTANCE / max_exact))
                         * np.float32(N_BUCKETS - max_exact)).astype(np.int32)
    large = np.minimum(large, N_BUCKETS - 1)
    return np.where(n < max_exact, n, large).astype(np.int32)


def _bias_lookup_kernel(rb_ref, idx_ref, o_ref, *, shift_bucket):
    idx = idx_ref[...]

    def body(b, accs):
        hit = idx == b
        return tuple(jnp.where(hit, rb_ref[b * N_HEADS + h], accs[h]) for h in range(N_HEADS))

    accs = lax.fori_loop(0, N_BUCKETS, body, tuple(jnp.zeros(idx.shape, F32) for _ in range(N_HEADS)))
    for h in range(N_HEADS):
        shift = 0.0 if shift_bucket is None else rb_ref[shift_bucket * N_HEADS + h]
        o_ref[h] = accs[h] - shift


def _bias_lookup(rel_bias, idx, shift_bucket):
    nt, r, c = idx.shape
    return pl.pallas_call(
        functools.partial(_bias_lookup_kernel, shift_bucket=shift_bucket),
        grid=(nt,),
        in_specs=[pl.BlockSpec(memory_space=pltpu.SMEM), pl.BlockSpec((None, r, c), lambda t: (t, 0, 0))],
        out_specs=pl.BlockSpec((N_HEADS, None, r, c), lambda t: (0, t, 0, 0)),
        out_shape=jax.ShapeDtypeStruct((N_HEADS, nt, r, c), F32),
        compiler_params=_params(("parallel",)),
        name="rel_bias_tables",
    )(rel_bias.reshape(-1).astype(F32), jnp.asarray(idx))


def _bias_tables(rel_bias, t_new):
    kk = np.arange(KB)[:, None]
    qq = np.arange(QB)[None, :]
    pidx = np.stack([_bucket_table((t - (KPQ - 1)) * KB + qq - kk) for t in range(KPQ + 1)])
    far_bucket = int(_bucket_table(np.array([2 * KB + 1]))[0])
    ptbl = _bias_lookup(rel_bias, pidx, far_bucket)
    c = np.arange(BLK)[None, :]
    t = np.arange(t_new)[:, None]
    sidx = np.stack([_bucket_table(np.full((t_new, BLK), 2 * BLK)), _bucket_table(BLK + t - c),
                     _bucket_table(t - c)])
    stbl = jnp.transpose(_bias_lookup(rel_bias, sidx, None), (1, 0, 2, 3)).reshape(3, N_HEADS * t_new, BLK)
    return ptbl, stbl


def _cmul(a, b):
    return a[0] * b[0] - a[1] * b[1], a[0] * b[1] + a[1] * b[0]


def _ssm_params(a_re, a_im, log_dt, b_re, b_im, c_re, c_im):
    nl = a_re.shape[0]
    dt = jnp.exp(log_dt)[..., None]
    decay = jnp.exp(dt * a_re)
    ab_re = decay * jnp.cos(dt * a_im)
    ab_im = decay * jnp.sin(dt * a_im)
    den = a_re * a_re + a_im * a_im
    f_re = ((ab_re - 1.0) * a_re + ab_im * a_im) / den
    f_im = (ab_im * a_re - (ab_re - 1.0) * a_im) / den
    bb_re = f_re[..., None] * b_re - f_im[..., None] * b_im
    bb_im = f_re[..., None] * b_im + f_im[..., None] * b_re
    eye = jnp.eye(SSM_GROUPS, dtype=F32)
    pack_b = lambda b: jnp.einsum("lgpc,gh->lgchp", b, eye).reshape(nl, D_GROUP, SSM_LANES).astype(BF16)
    pack_c = lambda c: jnp.einsum("lgcp,gh->lgphc", c, eye).reshape(nl, SSM_LANES, D_GROUP).astype(BF16)
    a1 = (ab_re.reshape(nl, SSM_LANES), ab_im.reshape(nl, SSM_LANES))

    def powers(a, n):
        out = [a]
        for _ in range(n - 1):
            out.append(_cmul(out[-1], a))
        return out

    bp = powers(a1, SUB)
    t = jnp.arange(SUB)[None, :, None]
    gate = lambda a, s: jnp.where(t >= s, a[:, None, :], 0.0)
    pw_re = jnp.stack([p[0] for p in bp], axis=1)
    pw_im = jnp.stack([p[1] for p in bp], axis=1)
    coef = jnp.stack([gate(bp[0][0], 1), gate(bp[0][1], 1), gate(bp[1][0], 2), gate(bp[1][1], 2),
                      gate(bp[3][0], 4), gate(bp[3][1], 4), pw_re, pw_im], axis=1)
    ab = jnp.stack([a1[0], a1[1]], axis=1)
    return pack_b(bb_re), pack_b(bb_im), pack_c(c_re), pack_c(c_im), coef, ab


def kernel(x_prompt, x_sample, cache_k, cache_v, cache_kidx, page_table, state_ssm_re, state_ssm_im, state_conv, state_pool, meta_tokens, rel_bias, norm_ffn1, ffn1_w_gate, ffn1_w_up, ffn1_w_down, norm_mix, w_in, w_out, ssm_a_re, ssm_a_im, ssm_log_dt, ssm_b_re, ssm_b_im, ssm_c_re, ssm_c_im, ssm_d, ssm_w_glu, conv_w, conv_b, conv_ln_g, conv_ln_b, pool_w, pool_scale, norm_ffn2, ffn2_w_gate, ffn2_w_up, ffn2_w_down, norm_final):
    nb, seq, d = x_prompt.shape
    nseq, t_new, _ = x_sample.shape
    depth = w_in.shape[0]
    n_pages = page_table.shape[1]
    n_pool = cache_k.shape[1]
    t_real = seq + N_META
    lp = -(-t_real // QB) * QB
    past = n_pages * PAGE
    n_sel_p = min(TOPK_MAX, t_real // 4)
    n_sel_s = min(TOPK_MAX, (past + t_new) // 4)
    mp, ms = nb * lp, nseq * t_new
    assert t_new == SUB and nseq % SEQ_PER_STEP == 0 and ms % QB == 0

    meta = jnp.broadcast_to(meta_tokens[None].astype(F32), (nb, N_META, d))
    xp = jnp.concatenate([meta, x_prompt, jnp.zeros((nb, lp - t_real, d), F32)], axis=1).reshape(mp, d)
    xs = x_sample.reshape(ms, d)

    zc = lambda n: jnp.zeros((depth, d, n), F32)
    q_, k_, v_, qi_, ki_, wi_, rest = (w_in[..., 0:256], w_in[..., 256:512], w_in[..., 512:768], w_in[..., 768:1024],
                                       w_in[..., 1024:1088], w_in[..., 1088:1092], w_in[..., 1092:])
    w_in_p = jnp.concatenate([q_, k_, v_, qi_, ki_, zc(64), zc(64), ki_, wi_, zc(124), rest], axis=-1).astype(BF16)
    g_ffn1 = norm_ffn1.reshape(depth, 1, d)
    g_ffn2 = norm_ffn2.reshape(depth, 1, d)
    g_mix = norm_mix.reshape(depth, 1, d)
    bbr, bbi, cre, cim, coef, ab = _ssm_params(ssm_a_re, ssm_a_im, ssm_log_dt, ssm_b_re, ssm_b_im, ssm_c_re, ssm_c_im)
    zrow = jnp.zeros((depth, D_GROUP), F32)
    vec = jnp.stack([ssm_d, conv_b, conv_ln_g, conv_ln_b, pool_scale, zrow, zrow, zrow], axis=1)
    cw = jnp.concatenate([conv_w, jnp.zeros((depth, 1, D_GROUP), F32)], axis=1)
    ng = len(POOL_WINDOWS)
    pw = jnp.einsum("lgcd,gh->lgchd", pool_w, jnp.eye(ng, dtype=F32)).reshape(depth, D_GROUP, D_GROUP).astype(BF16)
    wglu = ssm_w_glu.astype(BF16)
    ptbl, stbl = _bias_tables(rel_bias, t_new)

    ck = jnp.transpose(cache_k, (0, 1, 3, 4, 2)).reshape(depth, n_pool, D_GROUP, PAGE)
    cv = jnp.transpose(cache_v, (0, 1, 3, 4, 2)).reshape(depth, n_pool, D_GROUP, PAGE)
    cki = jnp.transpose(cache_kidx, (0, 1, 3, 2))
    pt_flat = page_table.reshape(-1).astype(I32)
    h0r = state_ssm_re.reshape(depth, nseq, SSM_LANES)
    h0i = state_ssm_im.reshape(depth, nseq, SSM_LANES)
    cst = state_conv.reshape(depth, nseq, (CONV_WIDTH - 1) * D_GROUP)
    pst = state_pool.reshape(depth, nseq, POOL_BUF * D_GROUP)
    names = ("k", "v", "ki", "wi", "qa", "qb", "kb", "vb", "qib", "kia", "kib", "us", "ca", "cg", "pp")

    pst_out = [[] for _ in range(7)]
    sst_out = [[] for _ in range(7)]
    for l in range(depth):
        xp = _ffn(xp, g_ffn1, ffn1_w_gate, ffn1_w_up, ffn1_w_down, l)
        xs = _ffn(xs, g_ffn1, ffn1_w_gate, ffn1_w_up, ffn1_w_down, l)

        pr = dict(zip(names, _proj(xp, g_mix, w_in_p, l)))
        sm = dict(zip(names, _proj(xs, g_mix, w_in_p, l)))
        sp = dict(bbr=bbr[l], bbi=bbi[l], cre=cre[l], cim=cim[l], coef=coef[l], ab=ab[l], vec=vec[l],
                  wglu=wglu[l], cw=cw[l], pw=pw[l])

        ya_p = _prompt_attention(pr, ptbl, nb, lp, n_sel_p)
        ys_p, yc_p, yp_p, u_p, hfin = _prompt_mixers(pr, sp, nb, lp, t_real)
        xp = _outproj(xp, (ya_p, ys_p, yc_p, yp_p), w_out, l)

        scores = _sample_scores(sm, cki, pt_flat, l, nseq, t_new, n_pages)
        am = _sample_select(scores.reshape(ms, -1), n_sel_s)
        qs = sm["qa"] + sm["qb"]
        ya_s = _sample_attention(sm, qs, am, stbl, ck, cv, pt_flat, l, nseq, t_new, n_pages)
        ys_s, yc_s, yp_s, u_s, hr_s, hi_s = _sample_mixers(sm, sp, h0r[l], h0i[l], cst[l], pst[l], nseq, t_new)
        wide = lambda a: a.reshape(ms, D_GROUP)
        xs = _outproj(xs, (ya_s, wide(ys_s), wide(yc_s), wide(yp_s)), w_out, l)

        xp = _ffn(xp, g_ffn2, ffn2_w_gate, ffn2_w_up, ffn2_w_down, l)
        xs = _ffn(xs, g_ffn2, ffn2_w_gate, ffn2_w_up, ffn2_w_down, l)

        seq3 = lambda a: a.reshape(nb, lp, -1)[:, :t_real]
        fin_row = (t_real - 1) % SUB
        pst_out[0].append(seq3(pr["k"]).reshape(nb, t_real, N_HEADS, HEAD_DIM))
        pst_out[1].append(seq3(pr["v"]).reshape(nb, t_real, N_HEADS, HEAD_DIM))
        pst_out[2].append(seq3(pr["ki"])[..., :IDX_DIM])
        pst_out[3].append(hfin[:, 0, fin_row].reshape(nb, SSM_GROUPS, SSM_STATE))
        pst_out[4].append(hfin[:, 1, fin_row].reshape(nb, SSM_GROUPS, SSM_STATE))
        pst_out[5].append(seq3(u_p)[:, t_real - (CONV_WIDTH - 1):])
        pst_out[6].append(seq3(pr["pp"])[:, t_real - POOL_BUF:])
        sst_out[0].append(sm["k"].reshape(nseq, t_new, N_HEADS, HEAD_DIM))
        sst_out[1].append(sm["v"].reshape(nseq, t_new, N_HEADS, HEAD_DIM))
        sst_out[2].append(sm["ki"][:, :IDX_DIM].reshape(nseq, t_new, IDX_DIM))
        sst_out[3].append(hr_s.reshape(nseq, SSM_GROUPS, SSM_STATE))
        sst_out[4].append(hi_s.reshape(nseq, SSM_GROUPS, SSM_STATE))
        sst_out[5].append(jnp.concatenate([state_conv[l][:, t_new:], u_s.reshape(nseq, t_new, D_GROUP)], axis=1))
        sst_out[6].append(jnp.concatenate([state_pool[l][:, t_new:], sm["pp"].reshape(nseq, t_new, D_GROUP)], axis=1))

    gf = norm_final[None]
    y_prompt = _rmsnorm(xp, gf).reshape(nb, lp, d)[:, N_META:t_real]
    y_sample = _rmsnorm(xs, gf).reshape(nseq, t_new, d)
    return (y_prompt, y_sample, *[jnp.stack(a) for a in pst_out], *[jnp.stack(a) for a in sst_out])
```

```python
import functools
import math

import numpy as np
import jax
import jax.numpy as jnp
from jax import lax
from jax.experimental import pallas as pl
from jax.experimental.pallas import tpu as pltpu

F32 = jnp.float32
BF16 = jnp.bfloat16
I32 = jnp.int32
I16 = jnp.int16

N_META = 16
N_HEADS = 4
HEAD_DIM = 64
IDX_HEADS = 4
IDX_DIM = 64
D_GROUP = 256
TOPK_MAX = 256
N_BUCKETS = 32
MAX_DISTANCE = 128
PAGE = 128
SEQ_PER_STEP = 4
SSM_GROUPS = 16
SSM_GROUP_CH = 16
SSM_STATE = 64
SSM_LANES = SSM_GROUPS * SSM_STATE
CONV_WIDTH = 31
POOL_WINDOWS = (2, 4, 8, 16)
POOL_BUF = 15
EPS = 1e-6

BLK = 128
SUB = 8
QB = 256
KB = 256
KPQ = QB // KB
HALF_ROWS = 16
HALF_BIAS = 32768
NEG = -1e30
KEY_NEG_INF = -2139095041
V7X_VMEM_BYTES = 64 * 1024 * 1024
VMEM_LIMIT = V7X_VMEM_BYTES - 8 * 1024 * 1024
TOKEN_TILE = 1024
PROJ_TILE = 512
FF_TILE = 512
PROJ_COLS = 2432


def _row_tile(m, target):
    tm = min(m, target)
    assert m % tm == 0 and tm % SUB == 0, (m, target)
    return tm


def _dot(a, b):
    return jnp.dot(a, b, preferred_element_type=F32)


def _dot_nt(a, b):
    return lax.dot_general(a, b, (((1,), (1,)), ((), ())), preferred_element_type=F32)


def _rms_bf16(x, g):
    ms = jnp.mean(x * x, axis=-1, keepdims=True)
    return (x * lax.rsqrt(ms + EPS) * g).astype(BF16)


def _sigmoid(x):
    return 1.0 / (1.0 + jnp.exp(-x))


def _gelu_tanh(x):
    c = math.sqrt(2.0 / math.pi)
    return 0.5 * x * (1.0 + jnp.tanh(c * (x + 0.044715 * (x * x * x))))


def _params(sem):
    return pltpu.CompilerParams(dimension_semantics=sem, vmem_limit_bytes=VMEM_LIMIT)


def _ffn_kernel(x_ref, g_ref, wg_ref, wu_ref, wd_ref, o_ref, h_scr, acc_scr):
    j = pl.program_id(1)

    @pl.when(j == 0)
    def _():
        h_scr[...] = _rms_bf16(x_ref[...], g_ref[...])
        acc_scr[...] = jnp.zeros_like(acc_scr)

    h = h_scr[...]
    g = _dot(h, wg_ref[...].astype(BF16))
    u = _dot(h, wu_ref[...].astype(BF16))
    a = (g * _sigmoid(g) * u).astype(BF16)
    acc_scr[...] += _dot(a, wd_ref[...].astype(BF16))

    @pl.when(j == pl.num_programs(1) - 1)
    def _():
        o_ref[...] = x_ref[...] + 0.5 * acc_scr[...]


def _ffn(x, g, wg, wu, wd, layer):
    m, d = x.shape
    tm, tf = _row_tile(m, TOKEN_TILE), FF_TILE
    dff = wg.shape[2]
    return pl.pallas_call(
        _ffn_kernel,
        grid=(m // tm, dff // tf),
        in_specs=[
            pl.BlockSpec((tm, d), lambda i, j: (i, 0)),
            pl.BlockSpec((None, 1, d), lambda i, j: (layer, 0, 0)),
            pl.BlockSpec((None, d, tf), lambda i, j: (layer, 0, j)),
            pl.BlockSpec((None, d, tf), lambda i, j: (layer, 0, j)),
            pl.BlockSpec((None, tf, d), lambda i, j: (layer, j, 0)),
        ],
        out_specs=pl.BlockSpec((tm, d), lambda i, j: (i, 0)),
        out_shape=jax.ShapeDtypeStruct((m, d), F32),
        scratch_shapes=[pltpu.VMEM((tm, d), BF16), pltpu.VMEM((tm, d), F32)],
        compiler_params=_params(("parallel", "arbitrary")),
        name="half_ffn",
    )(x, g, wg, wu, wd)


def _proj_kernel(x_ref, g_ref, w_ref, k_ref, v_ref, ki_ref, wi_ref, qa_ref, qb_ref, kb_ref, vb_ref,
                 qib_ref, kia_ref, kib_ref, us_ref, ca_ref, cg_ref, pp_ref):
    h = _rms_bf16(x_ref[...], g_ref[...])
    z = _dot(h, w_ref[...])
    q = z[:, 0:256] * (HEAD_DIM ** -0.5)
    lane = lax.broadcasted_iota(I32, q.shape, 1)
    even = ((lane // HEAD_DIM) % 2) == 0
    qa_ref[...] = jnp.where(even, q, 0.0).astype(BF16)
    qb_ref[...] = jnp.where(even, 0.0, q).astype(BF16)
    k = z[:, 256:512]
    v = z[:, 512:768]
    k_ref[...] = k
    v_ref[...] = v
    kb_ref[...] = k.astype(BF16)
    vb_ref[...] = v.astype(BF16)
    qib_ref[...] = z[:, 768:1024].astype(BF16)
    kia = z[:, 1024:1152]
    ki_ref[...] = kia
    kia_ref[...] = kia.astype(BF16)
    kib_ref[...] = z[:, 1152:1280].astype(BF16)
    wi_ref[...] = z[:, 1280:1408]
    us_ref[...] = z[:, 1408:1664]
    ca_ref[...] = z[:, 1664:1920]
    cg_ref[...] = z[:, 1920:2176]
    pp_ref[...] = z[:, 2176:2432]


def _proj(x, g, w, layer):
    m, d = x.shape
    tm = _row_tile(m, PROJ_TILE)
    row = lambda i: (i, 0)
    f256 = jax.ShapeDtypeStruct((m, 256), F32)
    f128 = jax.ShapeDtypeStruct((m, 128), F32)
    b256 = jax.ShapeDtypeStruct((m, 256), BF16)
    b128 = jax.ShapeDtypeStruct((m, 128), BF16)
    shapes = [f256, f256, f128, f128, b256, b256, b256, b256, b256, b128, b128, f256, f256, f256, f256]
    return pl.pallas_call(
        _proj_kernel,
        grid=(m // tm,),
        in_specs=[
            pl.BlockSpec((tm, d), row),
            pl.BlockSpec((None, 1, d), lambda i: (layer, 0, 0)),
            pl.BlockSpec((None, d, PROJ_COLS), lambda i: (layer, 0, 0)),
        ],
        out_specs=[pl.BlockSpec((tm, s.shape[1]), row) for s in shapes],
        out_shape=shapes,
        compiler_params=_params(("parallel",)),
        name="mixer_in_proj",
    )(x, g, w)


def _sortable_key(s):
    b = lax.bitcast_convert_type(s, I32)
    k = b ^ ((b >> 31) & 0x7FFFFFFF)
    return jnp.where(k == -1, 0, k)


def _fold(x, op):
    x3 = x.reshape(x.shape[0] // SUB, SUB, x.shape[1])
    return op(x3, axis=0)


def _tree_sum(parts):
    while len(parts) > 1:
        parts = [parts[n] + parts[n + 1] for n in range(0, len(parts) - 1, 2)] + parts[len(parts) & ~1:]
    return parts[0]


def _count16(ref, nkb, ref_val, strict):
    nq = ref.shape[2]
    cs = jnp.broadcast_to(ref_val, (HALF_ROWS, nq)).astype(I16)[None]
    one, zero = jnp.int16(1), jnp.int16(0)

    def blk(j, pc):
        x3 = ref[j].reshape(KB // HALF_ROWS, HALF_ROWS, nq)
        m3 = jnp.where(x3 > cs if strict else x3 >= cs, one, zero)
        return pc + _tree_sum([m3[g] for g in range(KB // HALF_ROWS)])

    pc = lax.fori_loop(0, nkb, blk, jnp.zeros((HALF_ROWS, nq), I16))
    return jnp.sum(pc.astype(I32), axis=0, keepdims=True)


def _nth_largest16(ref, nkb, n_need):
    nq = ref.shape[2]

    def bit_body(bi, t):
        c = t | jnp.left_shift(jnp.int32(1), 15 - bi)
        return jnp.where(_count16(ref, nkb, c - HALF_BIAS, False) >= n_need, c, t)

    return lax.fori_loop(0, 16, bit_body, jnp.zeros((1, nq), I32)) - HALF_BIAS


def _select_mask_t(ks_ref, hi_ref, lo_ref, am_ref, nkb, n_sel):
    nq = ks_ref.shape[2]
    t_hi = _nth_largest16(hi_ref, nkb, n_sel)
    n_above = _count16(hi_ref, nkb, t_hi, True)
    t_hi16 = jnp.broadcast_to(t_hi, (HALF_ROWS, nq)).astype(I16)[None]

    def park(j, carry):
        shape3 = (KB // HALF_ROWS, HALF_ROWS, nq)
        lo3 = jnp.where(hi_ref[j].reshape(shape3) == t_hi16, lo_ref[j].reshape(shape3), jnp.int16(-HALF_BIAS))
        lo_ref[j] = lo3.reshape(KB, nq)
        return carry

    lax.fori_loop(0, nkb, park, 0)
    t_lo = _nth_largest16(lo_ref, nkb, n_sel - n_above)
    thr = jnp.left_shift(t_hi, 16) | (t_lo + HALF_BIAS)
    need = (n_sel - n_above - _count16(lo_ref, nkb, t_lo, True)).astype(F32)
    r = lax.broadcasted_iota(I32, (KB, KB), 0)
    c = lax.broadcasted_iota(I32, (KB, KB), 1)
    tri = jnp.where(c <= r, 1.0, 0.0).astype(BF16)

    def mask_blk(j, seen):
        kk = ks_ref[j]
        eq = kk == thr
        pre = _dot(tri, jnp.where(eq, 1.0, 0.0).astype(BF16))
        tie = jnp.where(eq, jnp.where(seen + pre <= need, 0.0, NEG), NEG)
        am = jnp.where(kk > thr, 0.0, tie)
        am_ref[j] = jnp.where(kk == KEY_NEG_INF, NEG, am)
        return seen + pre[KB - 1:KB, :]

    lax.fori_loop(0, nkb, mask_blk, jnp.zeros((1, nq), F32))


def _pattn_kernel(qa_ref, qb_ref, qib_ref, wi_ref, kb_ref, vb_ref, kia_ref, kib_ref, tbl_ref, o_ref,
                  ks_scr, hi_scr, lo_scr, am_scr, vt_scr, acc_scr, p_scr, m_scr, *, n_sel):
    i = pl.program_id(1)
    n_full = i * KPQ
    nkb = n_full + KPQ
    half = D_GROUP // 2

    @pl.when(i == 0)
    def _():
        def tr(j, carry):
            ksl = pl.ds(pl.multiple_of(j * KB, KB), KB)
            vt_scr[j] = vb_ref[ksl, :].astype(F32).T.astype(BF16)
            return carry
        lax.fori_loop(0, vt_scr.shape[0], tr, 0)

    wit = wi_ref[...].T
    q01 = qib_ref[:, 0:half]
    q23 = qib_ref[:, half:D_GROUP]
    kpos0 = lax.broadcasted_iota(I32, (KB, QB), 0)
    qpos = i * QB + lax.broadcasted_iota(I32, (KB, QB), 1)

    def score_blk(diagonal):
        def body(j, carry):
            ksl = pl.ds(pl.multiple_of(j * KB, KB), KB)
            ka = kia_ref[ksl, :]
            kb2 = kib_ref[ksl, :]
            s = wit[0:1, :] * jnp.maximum(_dot_nt(ka, q01), 0.0)
            s = s + wit[1:2, :] * jnp.maximum(_dot_nt(kb2, q01), 0.0)
            s = s + wit[2:3, :] * jnp.maximum(_dot_nt(ka, q23), 0.0)
            s = s + wit[3:4, :] * jnp.maximum(_dot_nt(kb2, q23), 0.0)
            if diagonal:
                s = jnp.where(j * KB + kpos0 <= qpos, s, -jnp.inf)
            key = _sortable_key(s)
            ks_scr[j] = key
            hi_scr[j] = (key >> 16).astype(I16)
            lo_scr[j] = ((key & 0xFFFF) - HALF_BIAS).astype(I16)
            return carry
        return body

    lax.fori_loop(0, n_full, score_blk(False), 0)
    for d in range(KPQ):
        score_blk(True)(n_full + d, 0)
    _select_mask_t(ks_scr, hi_scr, lo_scr, am_scr, nkb, n_sel)

    qh = (qa_ref[:, 0:half], qb_ref[:, 0:half], qa_ref[:, half:D_GROUP], qb_ref[:, half:D_GROUP])

    def p_blk(near):
        def body(j, carry):
            ms, ls = carry
            ksl = pl.ds(pl.multiple_of(j * KB, KB), KB)
            new_m, new_l = [], []
            for h in range(N_HEADS):
                kp = kb_ref[ksl, 0:half] if h < 2 else kb_ref[ksl, half:D_GROUP]
                lg = _dot_nt(kp, qh[h]) + am_scr[j]
                if near:
                    lg = lg + tbl_ref[h, nkb - 1 - j]
                m_new = jnp.maximum(ms[h], jnp.max(_fold(lg, jnp.max), axis=0, keepdims=True))
                p = jnp.exp(lg - m_new)
                new_l.append(ls[h] * jnp.exp(ms[h] - m_new) + _fold(p, jnp.sum))
                new_m.append(m_new)
                p_scr[h, j] = p.astype(BF16)
                m_scr[h, j] = jnp.broadcast_to(m_new, (SUB, QB))
            return tuple(new_m), tuple(new_l)
        return body

    n_far = jnp.maximum(n_full - 1, 0)
    carry = (tuple(jnp.full((1, QB), -3e38, F32) for _ in range(N_HEADS)),
             tuple(jnp.zeros((SUB, QB), F32) for _ in range(N_HEADS)))
    carry = lax.fori_loop(0, n_far, p_blk(False), carry)
    ms, ls = lax.fori_loop(n_far, nkb, p_blk(True), carry)
    acc_scr[...] = jnp.zeros_like(acc_scr)

    def pv_blk(j, carry):
        for h in range(N_HEADS):
            vt = vt_scr[j, 0:half, :] if h < 2 else vt_scr[j, half:D_GROUP, :]
            acc_scr[h] += _dot(vt, p_scr[h, j]) * jnp.exp(m_scr[h, j, 0:1, :] - ms[h])
        return carry

    lax.fori_loop(0, nkb, pv_blk, 0)
    rows = []
    for h in range(N_HEADS):
        lo = (h % 2) * HEAD_DIM
        rows.append(acc_scr[h, lo:lo + HEAD_DIM, :] / jnp.sum(ls[h], axis=0, keepdims=True))
    o_ref[...] = jnp.concatenate(rows, axis=0).T


def _prompt_attention(pr, tbl, nb, lp, n_sel):
    nqb = lp // QB
    nkt = lp // KB
    m = nb * lp
    qblk = lambda c: pl.BlockSpec((QB, c), lambda b, i: (b * nqb + i, 0))
    kres = lambda c: pl.BlockSpec((lp, c), lambda b, i: (b, 0))
    return pl.pallas_call(
        functools.partial(_pattn_kernel, n_sel=n_sel),
        grid=(nb, nqb),
        in_specs=[qblk(256), qblk(256), qblk(256), qblk(128), kres(256), kres(256), kres(128), kres(128),
                  pl.BlockSpec((N_HEADS, KPQ + 1, KB, QB), lambda b, i: (0, 0, 0, 0))],
        out_specs=qblk(256),
        out_shape=jax.ShapeDtypeStruct((m, 256), F32),
        scratch_shapes=[pltpu.VMEM((nkt, KB, QB), I32), pltpu.VMEM((nkt, KB, QB), I16),
                        pltpu.VMEM((nkt, KB, QB), I16), pltpu.VMEM((nkt, KB, QB), F32),
                        pltpu.VMEM((nkt, D_GROUP, KB), BF16), pltpu.VMEM((N_HEADS, D_GROUP // 2, QB), F32),
                        pltpu.VMEM((N_HEADS, nkt, KB, QB), BF16), pltpu.VMEM((N_HEADS, nkt, SUB, QB), F32)],
        compiler_params=_params(("parallel", "arbitrary")),
        name="prompt_sparse_attention",
    )(pr["qa"], pr["qb"], pr["qib"], pr["wi"], pr["kb"], pr["vb"], pr["kia"], pr["kib"], tbl)


def _ssm_tail(y_lin, u, vec_ref, wglu_ref):
    y = y_lin + vec_ref[0:1, :] * u
    z = _gelu_tanh(y)
    return z * _sigmoid(_dot(z.astype(BF16), wglu_ref[...]))


def _conv_tail(acc, vec_ref):
    y = acc + vec_ref[1:2, :]
    mu = jnp.mean(y, axis=-1, keepdims=True)
    var = jnp.mean(jnp.square(y - mu), axis=-1, keepdims=True)
    y = (y - mu) * lax.rsqrt(var + EPS) * vec_ref[2:3, :] + vec_ref[3:4, :]
    return y * _sigmoid(y)


def _pool_pick(sums, cnts, cur):
    lane = lax.broadcasted_iota(I32, cur.shape, 1)
    grp = lane // (D_GROUP // len(POOL_WINDOWS))
    d = sums[3] / cnts[3]
    for g in (2, 1, 0):
        d = jnp.where(grp == g, sums[g] / cnts[g], d)
    return d - cur


def _pmix_kernel(us_ref, ca_ref, cg_ref, pp_ref, bbr_ref, bbi_ref, cre_ref, cim_ref, coef_ref, vec_ref,
                 wglu_ref, cw_ref, pw_ref,
                 yssm_ref, yconv_ref, ypool_ref, u_ref, hfin_ref,
                 hr_scr, hi_scr, car_scr, cext_scr, pext_scr, *, fin_chunk, fin_grp):
    i = pl.program_id(1)
    rows = BLK

    @pl.when(i == 0)
    def _():
        car_scr[...] = jnp.zeros_like(car_scr)
        cext_scr[0:32, :] = jnp.zeros((32, D_GROUP), F32)
        pext_scr[0:16, :] = jnp.zeros((16, D_GROUP), F32)

    us = us_ref[...]
    ub = us.astype(BF16)
    hr_scr[...] = _dot(ub, bbr_ref[...])
    hi_scr[...] = _dot(ub, bbi_ref[...])
    k1r, k1i, k2r, k2i, k4r, k4i, pwr, pwi = (coef_ref[n] for n in range(8))

    def grp_body(r, carry):
        cr, ci = carry
        sl = pl.ds(pl.multiple_of(r * SUB, SUB), SUB)
        xr = hr_scr[sl, :]
        xi = hi_scr[sl, :]
        for sh, (ar, ai) in ((1, (k1r, k1i)), (2, (k2r, k2i)), (4, (k4r, k4i))):
            sr = pltpu.roll(xr, sh, 0)
            si = pltpu.roll(xi, sh, 0)
            xr, xi = xr + ar * sr - ai * si, xi + ar * si + ai * sr
        hr = xr + pwr * cr - pwi * ci
        hi = xi + pwr * ci + pwi * cr
        hr_scr[sl, :] = hr
        hi_scr[sl, :] = hi
        return (jnp.broadcast_to(hr[SUB - 1:SUB, :], hr.shape), jnp.broadcast_to(hi[SUB - 1:SUB, :], hi.shape))

    cr, ci = lax.fori_loop(0, rows // SUB, grp_body, (car_scr[0], car_scr[1]))
    car_scr[0] = cr
    car_scr[1] = ci

    @pl.when(i == fin_chunk)
    def _():
        hfin_ref[0] = hr_scr[fin_grp * SUB:(fin_grp + 1) * SUB, :]
        hfin_ref[1] = hi_scr[fin_grp * SUB:(fin_grp + 1) * SUB, :]

    y_lin = _dot(hr_scr[...].astype(BF16), cre_ref[...]) - _dot(hi_scr[...].astype(BF16), cim_ref[...])
    yssm_ref[...] = _ssm_tail(y_lin, us, vec_ref, wglu_ref)

    u = ca_ref[...] * _sigmoid(cg_ref[...])
    u_ref[...] = u
    cext_scr[32:32 + rows, :] = u
    acc = jnp.zeros((rows, D_GROUP), F32)
    for k in range(CONV_WIDTH):
        acc = acc + cw_ref[k:k + 1, :] * cext_scr[2 + k:2 + k + rows, :]
    yconv_ref[...] = _conv_tail(acc, vec_ref)
    cext_scr[0:32, :] = cext_scr[rows:rows + 32, :]

    p = pp_ref[...]
    pext_scr[16:16 + rows, :] = p
    e = pext_scr[...]
    s2 = e + pltpu.roll(e, 1, 0)
    s4 = s2 + pltpu.roll(s2, 2, 0)
    s8 = s4 + pltpu.roll(s4, 4, 0)
    s16 = s8 + pltpu.roll(s8, 8, 0)
    pos1 = (i * rows + 1 + lax.broadcasted_iota(I32, (rows, D_GROUP), 0)).astype(F32)
    sums = [s[16:16 + rows, :] for s in (s2, s4, s8, s16)]
    cnts = [jnp.minimum(pos1, float(w)) for w in POOL_WINDOWS]
    d = _pool_pick(sums, cnts, p)
    ypool_ref[...] = _dot(d.astype(BF16), pw_ref[...]) * vec_ref[4:5, :]
    pext_scr[0:16, :] = pext_scr[rows:rows + 16, :]


def _prompt_mixers(pr, sp, nb, lp, t_real):
    nqb = lp // BLK
    m = nb * lp
    blk = pl.BlockSpec((BLK, D_GROUP), lambda b, i: (b * nqb + i, 0))
    full = lambda a: pl.BlockSpec(a.shape, lambda b, i: (0,) * a.ndim)
    last = t_real - 1
    weights = [sp["bbr"], sp["bbi"], sp["cre"], sp["cim"], sp["coef"], sp["vec"], sp["wglu"], sp["cw"], sp["pw"]]
    f256 = jax.ShapeDtypeStruct((m, D_GROUP), F32)
    return pl.pallas_call(
        functools.partial(_pmix_kernel, fin_chunk=last // BLK, fin_grp=(last % BLK) // SUB),
        grid=(nb, nqb),
        in_specs=[blk, blk, blk, blk] + [full(w) for w in weights],
        out_specs=[blk, blk, blk, blk, pl.BlockSpec((None, 2, SUB, SSM_LANES), lambda b, i: (b, 0, 0, 0))],
        out_shape=[f256, f256, f256, f256, jax.ShapeDtypeStruct((nb, 2, SUB, SSM_LANES), F32)],
        scratch_shapes=[pltpu.VMEM((BLK, SSM_LANES), F32), pltpu.VMEM((BLK, SSM_LANES), F32),
                        pltpu.VMEM((2, SUB, SSM_LANES), F32), pltpu.VMEM((32 + BLK, D_GROUP), F32),
                        pltpu.VMEM((16 + BLK, D_GROUP), F32)],
        compiler_params=_params(("parallel", "arbitrary")),
        name="prompt_ssm_conv_pool",
    )(pr["us"], pr["ca"], pr["cg"], pr["pp"], *weights)


def _smix_kernel(us_ref, ca_ref, cg_ref, pp_ref, h0r_ref, h0i_ref, cst_ref, pst_ref,
                 bbr_ref, bbi_ref, cre_ref, cim_ref, ab_ref, vec_ref, wglu_ref, cw_ref, pw_ref,
                 yssm_ref, yconv_ref, ypool_ref, u_ref, hr_ref, hi_ref, *, t_new):
    c = D_GROUP
    sl = lambda t: slice(t * c, (t + 1) * c)
    ar = ab_ref[0:1, :]
    ai = ab_ref[1:2, :]
    hr = h0r_ref[...]
    hi = h0i_ref[...]
    for t in range(t_new):
        us = us_ref[:, sl(t)]
        ub = us.astype(BF16)
        hr, hi = ar * hr - ai * hi + _dot(ub, bbr_ref[...]), ar * hi + ai * hr + _dot(ub, bbi_ref[...])
        y_lin = _dot(hr.astype(BF16), cre_ref[...]) - _dot(hi.astype(BF16), cim_ref[...])
        yssm_ref[:, sl(t)] = _ssm_tail(y_lin, us, vec_ref, wglu_ref)
    hr_ref[...] = hr
    hi_ref[...] = hi

    n_cst = CONV_WIDTH - 1
    ext = [cst_ref[:, sl(j)] for j in range(n_cst)]
    for t in range(t_new):
        u = ca_ref[:, sl(t)] * _sigmoid(cg_ref[:, sl(t)])
        u_ref[:, sl(t)] = u
        ext.append(u)
    for t in range(t_new):
        acc = cw_ref[0:1, :] * ext[t]
        for k in range(1, CONV_WIDTH):
            acc = acc + cw_ref[k:k + 1, :] * ext[t + k]
        yconv_ref[:, sl(t)] = _conv_tail(acc, vec_ref)

    pe = [pst_ref[:, sl(j)] for j in range(POOL_BUF)] + [pp_ref[:, sl(t)] for t in range(t_new)]
    n = len(pe)
    s2 = [None] + [pe[j] + pe[j - 1] for j in range(1, n)]
    s4 = [None] * 3 + [s2[j] + s2[j - 2] for j in range(3, n)]
    s8 = [None] * 7 + [s4[j] + s4[j - 4] for j in range(7, n)]
    s16 = [None] * 15 + [s8[j] + s8[j - 8] for j in range(15, n)]
    for t in range(t_new):
        j = POOL_BUF + t
        cnts = [float(w) for w in POOL_WINDOWS]
        d = _pool_pick([s2[j], s4[j], s8[j], s16[j]], cnts, pe[j])
        ypool_ref[:, sl(t)] = _dot(d.astype(BF16), pw_ref[...]) * vec_ref[4:5, :]


def _sample_mixers(sm, sp, h0r, h0i, cst, pst, nseq, t_new):
    wide = lambda a: a.reshape(nseq, t_new * D_GROUP)
    args = [wide(sm["us"]), wide(sm["ca"]), wide(sm["cg"]), wide(sm["pp"]), h0r, h0i, cst, pst,
            sp["bbr"], sp["bbi"], sp["cre"], sp["cim"], sp["ab"], sp["vec"], sp["wglu"], sp["cw"], sp["pw"]]
    full = lambda a: pl.BlockSpec(a.shape, lambda i: (0,) * a.ndim)
    fw = jax.ShapeDtypeStruct((nseq, t_new * D_GROUP), F32)
    fs = jax.ShapeDtypeStruct((nseq, SSM_LANES), F32)
    shapes = [fw, fw, fw, fw, fs, fs]
    return pl.pallas_call(
        functools.partial(_smix_kernel, t_new=t_new),
        grid=(1,),
        in_specs=[full(a) for a in args],
        out_specs=[full(s) for s in shapes],
        out_shape=shapes,
        compiler_params=_params(("arbitrary",)),
        name="sample_ssm_conv_pool",
    )(*args)


def _sscore_kernel(pt_ref, qib_ref, wi_ref, kin_ref, *rest, n_pages, t_new):
    page_refs = rest[:SEQ_PER_STEP * n_pages]
    s_ref = rest[SEQ_PER_STEP * n_pages]
    for g in range(SEQ_PER_STEP):
        rows = slice(g * t_new, (g + 1) * t_new)
        qif = qib_ref[rows, :].astype(F32)
        qst = jnp.concatenate([qif[:, h * IDX_DIM:(h + 1) * IDX_DIM] for h in range(IDX_HEADS)], axis=0).astype(BF16)
        wcol = jnp.concatenate([wi_ref[rows, h:h + 1] for h in range(IDX_HEADS)], axis=0)

        def score(qk):
            d = wcol * jnp.maximum(qk, 0.0)
            s = d[0:t_new]
            for h in range(1, IDX_HEADS):
                s = s + d[h * t_new:(h + 1) * t_new]
            return s

        for p in range(n_pages):
            page = page_refs[g * n_pages + p][...].astype(BF16)
            s_ref[g, :, p * PAGE:(p + 1) * PAGE] = score(_dot(qst, page))
        knew = jnp.concatenate([kin_ref[rows, :].astype(F32)[:, 0:IDX_DIM],
                                jnp.zeros((PAGE - t_new, IDX_DIM), F32)], axis=0)
        s = score(_dot_nt(qst, knew.astype(BF16)))
        r = lax.broadcasted_iota(I32, s.shape, 0)
        c = lax.broadcasted_iota(I32, s.shape, 1)
        s_ref[g, :, n_pages * PAGE:(n_pages + 1) * PAGE] = jnp.where(c <= r, s, -jnp.inf)


def _page_specs(shape, layer, n_pages):
    def spec(g, p):
        return pl.BlockSpec((None, None) + shape,
                            lambda b, pt: (layer, pt[(b * SEQ_PER_STEP + g) * n_pages + p], 0, 0))
    return [spec(g, p) for g in range(SEQ_PER_STEP) for p in range(n_pages)]


def _sample_scores(sm, cache_kidx, pt_flat, layer, nseq, t_new, n_pages):
    ncol = (n_pages + 1) * PAGE
    row = lambda c: pl.BlockSpec((SEQ_PER_STEP * t_new, c), lambda b, pt: (b, 0))
    pages = _page_specs((IDX_DIM, PAGE), layer, n_pages)
    grid_spec = pltpu.PrefetchScalarGridSpec(
        num_scalar_prefetch=1, grid=(nseq // SEQ_PER_STEP,),
        in_specs=[row(256), row(128), row(128)] + pages,
        out_specs=pl.BlockSpec((SEQ_PER_STEP, t_new, ncol), lambda b, pt: (b, 0, 0)))
    return pl.pallas_call(
        functools.partial(_sscore_kernel, n_pages=n_pages, t_new=t_new),
        grid_spec=grid_spec,
        out_shape=jax.ShapeDtypeStruct((nseq, t_new, ncol), F32),
        compiler_params=_params(("parallel",)),
        name="sample_indexer_scores",
    )(pt_flat, sm["qib"], sm["wi"], sm["kia"], *([cache_kidx] * len(pages)))


def _sselect_kernel(s_ref, am_ref, ks_scr, hi_scr, lo_scr, am_scr, *, nkb, n_sel):
    for j in range(nkb):
        key = _sortable_key(s_ref[j * KB:(j + 1) * KB, :])
        ks_scr[j] = key
        hi_scr[j] = (key >> 16).astype(I16)
        lo_scr[j] = ((key & 0xFFFF) - HALF_BIAS).astype(I16)
    _select_mask_t(ks_scr, hi_scr, lo_scr, am_scr, nkb, n_sel)
    for j in range(nkb):
        am_ref[j * KB:(j + 1) * KB, :] = am_scr[j]


def _sample_select(scores, n_sel):
    m, ncol = scores.shape
    nkb = -(-ncol // KB)
    st = jnp.concatenate([scores, jnp.full((m, nkb * KB - ncol), -jnp.inf, F32)], axis=1).T
    spec = pl.BlockSpec((nkb * KB, QB), lambda i: (0, i))
    blk = lambda dt: pltpu.VMEM((nkb, KB, QB), dt)
    amt = pl.pallas_call(
        functools.partial(_sselect_kernel, nkb=nkb, n_sel=n_sel),
        grid=(m // QB,),
        in_specs=[spec],
        out_specs=spec,
        out_shape=jax.ShapeDtypeStruct((nkb * KB, m), F32),
        scratch_shapes=[blk(I32), blk(I16), blk(I16), blk(F32)],
        compiler_params=_params(("parallel",)),
        name="sample_topn_mask",
    )(st)
    return amt.T


def _sattn_kernel(pt_ref, q_ref, am_ref, kn_ref, vn_ref, tbl_ref, *rest, n_pages, t_new):
    n_refs = SEQ_PER_STEP * n_pages
    o_ref = rest[2 * n_refs]
    zpad = jnp.zeros((PAGE - t_new, D_GROUP), F32)
    for g in range(SEQ_PER_STEP):
        rows = slice(g * t_new, (g + 1) * t_new)
        k_refs = rest[g * n_pages:(g + 1) * n_pages]
        v_refs = rest[n_refs + g * n_pages:n_refs + (g + 1) * n_pages]
        q = q_ref[rows, :].astype(F32)
        lane = lax.broadcasted_iota(I32, q.shape, 1)
        qbd = jnp.concatenate([jnp.where(lane // HEAD_DIM == h, q, 0.0) for h in range(N_HEADS)], axis=0).astype(BF16)
        pad_page = lambda r: jnp.concatenate([r[rows, :].astype(F32), zpad], axis=0).astype(BF16)
        lgs = []
        for p in range(n_pages + 1):
            am = am_ref[rows, p * PAGE:(p + 1) * PAGE]
            am4 = jnp.concatenate([am] * N_HEADS, axis=0)
            tsel = 0 if p < n_pages - 1 else (1 if p == n_pages - 1 else 2)
            qk = _dot(qbd, k_refs[p][...].astype(BF16)) if p < n_pages else _dot_nt(qbd, pad_page(kn_ref))
            lgs.append(qk + tbl_ref[tsel] + am4)
        mx = lgs[0]
        for lg in lgs[1:]:
            mx = jnp.maximum(mx, lg)
        mx = jnp.max(mx, axis=-1, keepdims=True)
        ps = [jnp.exp(lg - mx) for lg in lgs]
        tot = ps[0]
        for pp_ in ps[1:]:
            tot = tot + pp_
        den = jnp.sum(tot, axis=-1, keepdims=True)
        acc = _dot(ps[n_pages].astype(BF16), pad_page(vn_ref))
        for p in range(n_pages):
            acc = acc + _dot_nt(ps[p].astype(BF16), v_refs[p][...].astype(BF16))
        acc = acc / den
        out = acc[0:t_new]
        for h in range(1, N_HEADS):
            out = jnp.where(lane // HEAD_DIM == h, acc[h * t_new:(h + 1) * t_new], out)
        o_ref[rows, :] = out


def _sample_attention(sm, qs, am, tbl, cache_k, cache_v, pt_flat, layer, nseq, t_new, n_pages):
    row = lambda c: pl.BlockSpec((SEQ_PER_STEP * t_new, c), lambda b, pt: (b, 0))
    pages = _page_specs((D_GROUP, PAGE), layer, n_pages)
    grid_spec = pltpu.PrefetchScalarGridSpec(
        num_scalar_prefetch=1, grid=(nseq // SEQ_PER_STEP,),
        in_specs=[row(256), row(am.shape[1]), row(256), row(256),
                  pl.BlockSpec(tbl.shape, lambda b, pt: (0, 0, 0))]
        + pages + _page_specs((D_GROUP, PAGE), layer, n_pages),
        out_specs=row(256))
    return pl.pallas_call(
        functools.partial(_sattn_kernel, n_pages=n_pages, t_new=t_new),
        grid_spec=grid_spec,
        out_shape=jax.ShapeDtypeStruct((nseq * t_new, D_GROUP), F32),
        compiler_params=_params(("parallel",)),
        name="sample_sparse_attention",
    )(pt_flat, qs, am, sm["kb"], sm["vb"], tbl, *([cache_k] * len(pages)), *([cache_v] * len(pages)))


def _outproj_kernel(x_ref, a_ref, s_ref, c_ref, p_ref, w_ref, o_ref):
    acc = x_ref[...]
    for n, r in enumerate((a_ref, s_ref, c_ref, p_ref)):
        acc = acc + _dot(r[...].astype(BF16), w_ref[n * D_GROUP:(n + 1) * D_GROUP, :].astype(BF16))
    o_ref[...] = acc


def _outproj(x, ys, w, layer):
    m, d = x.shape
    tm = _row_tile(m, TOKEN_TILE)
    row = lambda c: pl.BlockSpec((tm, c), lambda i: (i, 0))
    return pl.pallas_call(
        _outproj_kernel,
        grid=(m // tm,),
        in_specs=[row(d)] + [row(D_GROUP)] * 4 + [pl.BlockSpec((None,) + w.shape[1:], lambda i: (layer, 0, 0))],
        out_specs=row(d),
        out_shape=jax.ShapeDtypeStruct((m, d), F32),
        compiler_params=_params(("parallel",)),
        name="mixer_out_proj",
    )(x, *ys, w)


def _rmsnorm_kernel(x_ref, g_ref, o_ref):
    x = x_ref[...]
    ms = jnp.mean(x * x, axis=-1, keepdims=True)
    o_ref[...] = x * lax.rsqrt(ms + EPS) * g_ref[...]


def _rmsnorm(x, g):
    m, d = x.shape
    tm = _row_tile(m, TOKEN_TILE)
    return pl.pallas_call(
        _rmsnorm_kernel,
        grid=(m // tm,),
        in_specs=[pl.BlockSpec((tm, d), lambda i: (i, 0)), pl.BlockSpec((1, d), lambda i: (0, 0))],
        out_specs=pl.BlockSpec((tm, d), lambda i: (i, 0)),
        out_shape=jax.ShapeDtypeStruct((m, d), F32),
        compiler_params=_params(("parallel",)),
        name="final_rmsnorm",
    )(x, g)


def _bucket_table(dist):
    n = np.maximum(dist, 0)
    max_exact = N_BUCKETS // 2
    nf = np.maximum(n, 1).astype(np.float32)
    large = max_exact + (np.log(nf / np.float32(max_exact)) / np.float32(math.log(MAX_DISTANCE / max_exact))
                         * np.float32(N_BUCKETS - max_exact)).astype(np.int32)
    large = np.minimum(large, N_BUCKETS - 1)
    return np.where(n < max_exact, n, large).astype(np.int32)


def _bias_lookup_kernel(rb_ref, idx_ref, o_ref, *, shift_bucket):
    idx = idx_ref[...]

    def body(b, accs):
        hit = idx == b
        return tuple(jnp.where(hit, rb_ref[b * N_HEADS + h], accs[h]) for h in range(N_HEADS))

    accs = lax.fori_loop(0, N_BUCKETS, body, tuple(jnp.zeros(idx.shape, F32) for _ in range(N_HEADS)))
    for h in range(N_HEADS):
        shift = 0.0 if shift_bucket is None else rb_ref[shift_bucket * N_HEADS + h]
        o_ref[h] = accs[h] - shift


def _bias_lookup(rel_bias, idx, shift_bucket):
    nt, r, c = idx.shape
    return pl.pallas_call(
        functools.partial(_bias_lookup_kernel, shift_bucket=shift_bucket),
        grid=(nt,),
        in_specs=[pl.BlockSpec(memory_space=pltpu.SMEM), pl.BlockSpec((None, r, c), lambda t: (t, 0, 0))],
        out_specs=pl.BlockSpec((N_HEADS, None, r, c), lambda t: (0, t, 0, 0)),
        out_shape=jax.ShapeDtypeStruct((N_HEADS, nt, r, c), F32),
        compiler_params=_params(("parallel",)),
        name="rel_bias_tables",
    )(rel_bias.reshape(-1).astype(F32), jnp.asarray(idx))


def _bias_tables(rel_bias, t_new):
    kk = np.arange(KB)[:, None]
    qq = np.arange(QB)[None, :]
    pidx = np.stack([_bucket_table((t - (KPQ - 1)) * KB + qq - kk) for t in range(KPQ + 1)])
    far_bucket = int(_bucket_table(np.array([2 * KB + 1]))[0])
    ptbl = _bias_lookup(rel_bias, pidx, far_bucket)
    c = np.arange(BLK)[None, :]
    t = np.arange(t_new)[:, None]
    sidx = np.stack([_bucket_table(np.full((t_new, BLK), 2 * BLK)), _bucket_table(BLK + t - c),
                     _bucket_table(t - c)])
    stbl = jnp.transpose(_bias_lookup(rel_bias, sidx, None), (1, 0, 2, 3)).reshape(3, N_HEADS * t_new, BLK)
    return ptbl, stbl


def _cmul(a, b):
    return a[0] * b[0] - a[1] * b[1], a[0] * b[1] + a[1] * b[0]


def _ssm_params(a_re, a_im, log_dt, b_re, b_im, c_re, c_im):
    nl = a_re.shape[0]
    dt = jnp.exp(log_dt)[..., None]
    decay = jnp.exp(dt * a_re)
    ab_re = decay * jnp.cos(dt * a_im)
    ab_im = decay * jnp.sin(dt * a_im)
    den = a_re * a_re + a_im * a_im
    f_re = ((ab_re - 1.0) * a_re + ab_im * a_im) / den
    f_im = (ab_im * a_re - (ab_re - 1.0) * a_im) / den
    bb_re = f_re[..., None] * b_re - f_im[..., None] * b_im
    bb_im = f_re[..., None] * b_im + f_im[..., None] * b_re
    eye = jnp.eye(SSM_GROUPS, dtype=F32)
    pack_b = lambda b: jnp.einsum("lgpc,gh->lgchp", b, eye).reshape(nl, D_GROUP, SSM_LANES).astype(BF16)
    pack_c = lambda c: jnp.einsum("lgcp,gh->lgphc", c, eye).reshape(nl, SSM_LANES, D_GROUP).astype(BF16)
    a1 = (ab_re.reshape(nl, SSM_LANES), ab_im.reshape(nl, SSM_LANES))

    def powers(a, n):
        out = [a]
        for _ in range(n - 1):
            out.append(_cmul(out[-1], a))
        return out

    bp = powers(a1, SUB)
    t = jnp.arange(SUB)[None, :, None]
    gate = lambda a, s: jnp.where(t >= s, a[:, None, :], 0.0)
    pw_re = jnp.stack([p[0] for p in bp], axis=1)
    pw_im = jnp.stack([p[1] for p in bp], axis=1)
    coef = jnp.stack([gate(bp[0][0], 1), gate(bp[0][1], 1), gate(bp[1][0], 2), gate(bp[1][1], 2),
                      gate(bp[3][0], 4), gate(bp[3][1], 4), pw_re, pw_im], axis=1)
    ab = jnp.stack([a1[0], a1[1]], axis=1)
    return pack_b(bb_re), pack_b(bb_im), pack_c(c_re), pack_c(c_im), coef, ab


def kernel(x_prompt, x_sample, cache_k, cache_v, cache_kidx, page_table, state_ssm_re, state_ssm_im, state_conv, state_pool, meta_tokens, rel_bias, norm_ffn1, ffn1_w_gate, ffn1_w_up, ffn1_w_down, norm_mix, w_in, w_out, ssm_a_re, ssm_a_im, ssm_log_dt, ssm_b_re, ssm_b_im, ssm_c_re, ssm_c_im, ssm_d, ssm_w_glu, conv_w, conv_b, conv_ln_g, conv_ln_b, pool_w, pool_scale, norm_ffn2, ffn2_w_gate, ffn2_w_up, ffn2_w_down, norm_final):
    nb, seq, d = x_prompt.shape
    nseq, t_new, _ = x_sample.shape
    depth = w_in.shape[0]
    n_pages = page_table.shape[1]
    n_pool = cache_k.shape[1]
    t_real = seq + N_META
    lp = -(-t_real // QB) * QB
    past = n_pages * PAGE
    n_sel_p = min(TOPK_MAX, t_real // 4)
    n_sel_s = min(TOPK_MAX, (past + t_new) // 4)
    mp, ms = nb * lp, nseq * t_new
    assert t_new == SUB and nseq % SEQ_PER_STEP == 0 and ms % QB == 0

    meta = jnp.broadcast_to(meta_tokens[None].astype(F32), (nb, N_META, d))
    xp = jnp.concatenate([meta, x_prompt, jnp.zeros((nb, lp - t_real, d), F32)], axis=1).reshape(mp, d)
    xs = x_sample.reshape(ms, d)

    zc = lambda n: jnp.zeros((depth, d, n), F32)
    q_, k_, v_, qi_, ki_, wi_, rest = (w_in[..., 0:256], w_in[..., 256:512], w_in[..., 512:768], w_in[..., 768:1024],
                                       w_in[..., 1024:1088], w_in[..., 1088:1092], w_in[..., 1092:])
    w_in_p = jnp.concatenate([q_, k_, v_, qi_, ki_, zc(64), zc(64), ki_, wi_, zc(124), rest], axis=-1).astype(BF16)
    g_ffn1 = norm_ffn1.reshape(depth, 1, d)
    g_ffn2 = norm_ffn2.reshape(depth, 1, d)
    g_mix = norm_mix.reshape(depth, 1, d)
    bbr, bbi, cre, cim, coef, ab = _ssm_params(ssm_a_re, ssm_a_im, ssm_log_dt, ssm_b_re, ssm_b_im, ssm_c_re, ssm_c_im)
    zrow = jnp.zeros((depth, D_GROUP), F32)
    vec = jnp.stack([ssm_d, conv_b, conv_ln_g, conv_ln_b, pool_scale, zrow, zrow, zrow], axis=1)
    cw = jnp.concatenate([conv_w, jnp.zeros((depth, 1, D_GROUP), F32)], axis=1)
    ng = len(POOL_WINDOWS)
    pw = jnp.einsum("lgcd,gh->lgchd", pool_w, jnp.eye(ng, dtype=F32)).reshape(depth, D_GROUP, D_GROUP).astype(BF16)
    wglu = ssm_w_glu.astype(BF16)
    ptbl, stbl = _bias_tables(rel_bias, t_new)

    ck = jnp.transpose(cache_k, (0, 1, 3, 4, 2)).reshape(depth, n_pool, D_GROUP, PAGE)
    cv = jnp.transpose(cache_v, (0, 1, 3, 4, 2)).reshape(depth, n_pool, D_GROUP, PAGE)
    cki = jnp.transpose(cache_kidx, (0, 1, 3, 2))
    pt_flat = page_table.reshape(-1).astype(I32)
    h0r = state_ssm_re.reshape(depth, nseq, SSM_LANES)
    h0i = state_ssm_im.reshape(depth, nseq, SSM_LANES)
    cst = state_conv.reshape(depth, nseq, (CONV_WIDTH - 1) * D_GROUP)
    pst = state_pool.reshape(depth, nseq, POOL_BUF * D_GROUP)
    names = ("k", "v", "ki", "wi", "qa", "qb", "kb", "vb", "qib", "kia", "kib", "us", "ca", "cg", "pp")

    pst_out = [[] for _ in range(7)]
    sst_out = [[] for _ in range(7)]
    for l in range(depth):
        xp = _ffn(xp, g_ffn1, ffn1_w_gate, ffn1_w_up, ffn1_w_down, l)
        xs = _ffn(xs, g_ffn1, ffn1_w_gate, ffn1_w_up, ffn1_w_down, l)

        pr = dict(zip(names, _proj(xp, g_mix, w_in_p, l)))
        sm = dict(zip(names, _proj(xs, g_mix, w_in_p, l)))
        sp = dict(bbr=bbr[l], bbi=bbi[l], cre=cre[l], cim=cim[l], coef=coef[l], ab=ab[l], vec=vec[l],
                  wglu=wglu[l], cw=cw[l], pw=pw[l])

        ya_p = _prompt_attention(pr, ptbl, nb, lp, n_sel_p)
        ys_p, yc_p, yp_p, u_p, hfin = _prompt_mixers(pr, sp, nb, lp, t_real)
        xp = _outproj(xp, (ya_p, ys_p, yc_p, yp_p), w_out, l)

        scores = _sample_scores(sm, cki, pt_flat, l, nseq, t_new, n_pages)
        am = _sample_select(scores.reshape(ms, -1), n_sel_s)
        qs = sm["qa"] + sm["qb"]
        ya_s = _sample_attention(sm, qs, am, stbl, ck, cv, pt_flat, l, nseq, t_new, n_pages)
        ys_s, yc_s, yp_s, u_s, hr_s, hi_s = _sample_mixers(sm, sp, h0r[l], h0i[l], cst[l], pst[l], nseq, t_new)
        wide = lambda a: a.reshape(ms, D_GROUP)
        xs = _outproj(xs, (ya_s, wide(ys_s), wide(yc_s), wide(yp_s)), w_out, l)

        xp = _ffn(xp, g_ffn2, ffn2_w_gate, ffn2_w_up, ffn2_w_down, l)
        xs = _ffn(xs, g_ffn2, ffn2_w_gate, ffn2_w_up, ffn2_w_down, l)

        seq3 = lambda a: a.reshape(nb, lp, -1)[:, :t_real]
        fin_row = (t_real - 1) % SUB
        pst_out[0].append(seq3(pr["k"]).reshape(nb, t_real, N_HEADS, HEAD_DIM))
        pst_out[1].append(seq3(pr["v"]).reshape(nb, t_real, N_HEADS, HEAD_DIM))
        pst_out[2].append(seq3(pr["ki"])[..., :IDX_DIM])
        pst_out[3].append(hfin[:, 0, fin_row].reshape(nb, SSM_GROUPS, SSM_STATE))
        pst_out[4].append(hfin[:, 1, fin_row].reshape(nb, SSM_GROUPS, SSM_STATE))
        pst_out[5].append(seq3(u_p)[:, t_real - (CONV_WIDTH - 1):])
        pst_out[6].append(seq3(pr["pp"])[:, t_real - POOL_BUF:])
        sst_out[0].append(sm["k"].reshape(nseq, t_new, N_HEADS, HEAD_DIM))
        sst_out[1].append(sm["v"].reshape(nseq, t_new, N_HEADS, HEAD_DIM))
        sst_out[2].append(sm["ki"][:, :IDX_DIM].reshape(nseq, t_new, IDX_DIM))
        sst_out[3].append(hr_s.reshape(nseq, SSM_GROUPS, SSM_STATE))
        sst_out[4].append(hi_s.reshape(nseq, SSM_GROUPS, SSM_STATE))
        sst_out[5].append(jnp.concatenate([state_conv[l][:, t_new:], u_s.reshape(nseq, t_new, D_GROUP)], axis=1))
        sst_out[6].append(jnp.concatenate([state_pool[l][:, t_new:], sm["pp"].reshape(nseq, t_new, D_GROUP)], axis=1))

    gf = norm_final[None]
    y_prompt = _rmsnorm(xp, gf).reshape(nb, lp, d)[:, N_META:t_real]
    y_sample = _rmsnorm(xs, gf).reshape(nseq, t_new, d)
    return (y_prompt, y_sample, *[jnp.stack(a) for a in pst_out], *[jnp.stack(a) for a in sst_out])
```
